```python
import jax, jax.numpy as jnp
from jax import lax
import numpy as np

D_MODEL = 1024
BATCH = 8
SEQ = 4096
DEPTH = 2

PLE_DIM = 256
EPS = 1e-6

MLA_HEADS = 4
MLA_Q_RANK = 384
MLA_KV_RANK = 256
MLA_NOPE = 128
MLA_ROPE = 64
MLA_V = 128
MLA_WIDTH = MLA_HEADS * MLA_V
ROPE_THETA = 10000.0
Q_BLOCK = 128

GLA_HEADS = 4
GLA_DK = 64
GLA_DV = 128
GLA_WIDTH = GLA_HEADS * GLA_DV
GLA_GATE_RANK = 16
GLA_TAU = 16.0
GLA_CHUNK = 64

D_MIX = MLA_WIDTH + GLA_WIDTH

IN_SPLITS = (
    MLA_Q_RANK,
    MLA_KV_RANK,
    MLA_ROPE,
    MLA_WIDTH,
    GLA_HEADS * GLA_DK,
    GLA_HEADS * GLA_DK,
    GLA_WIDTH,
    GLA_GATE_RANK,
    GLA_GATE_RANK,
    GLA_WIDTH,
)
D_IN = sum(IN_SPLITS)

kernel_name = "hymba_mla_gla_bidir_encoder"


def rmsnorm(x, g):
    xf = x.astype(jnp.float32)
    y = xf * lax.rsqrt(jnp.mean(xf * xf, axis=-1, keepdims=True) + EPS)
    return (y * g.astype(jnp.float32)).astype(x.dtype)


def split_cols(u, sizes):
    out, start = [], 0
    for s in sizes:
        out.append(u[..., start:start + s])
        start += s
    return out


def apply_rope(x, pos):
    half = MLA_ROPE // 2
    inv = ROPE_THETA ** (-jnp.arange(half, dtype=jnp.float32) / half)
    ang = pos.astype(jnp.float32)[..., None] * inv
    cos, sin = jnp.cos(ang), jnp.sin(ang)
    if x.ndim == 4:
        cos, sin = cos[:, :, None, :], sin[:, :, None, :]
    xf = x.astype(jnp.float32)
    x1, x2 = xf[..., :half], xf[..., half:]
    return jnp.concatenate([x1 * cos - x2 * sin, x1 * sin + x2 * cos], axis=-1).astype(x.dtype)


def mla_branch(c_q, c_kv, k_rope, pos, q_norm, w_uq, kv_norm, w_ukv):
    B, S, _ = c_q.shape
    q = (rmsnorm(c_q, q_norm) @ w_uq).reshape(B, S, MLA_HEADS, MLA_NOPE + MLA_ROPE)
    q_nope = q[..., :MLA_NOPE]
    q_rope = apply_rope(q[..., MLA_NOPE:], pos)
    kv = (rmsnorm(c_kv, kv_norm) @ w_ukv).reshape(B, S, MLA_HEADS, MLA_NOPE + MLA_V)
    k_nope, v = kv[..., :MLA_NOPE], kv[..., MLA_NOPE:]
    k_r = apply_rope(k_rope, pos)
    scale = (MLA_NOPE + MLA_ROPE) ** -0.5
    nb = S // Q_BLOCK
    qn_b = q_nope.reshape(B, nb, Q_BLOCK, MLA_HEADS, MLA_NOPE).transpose(1, 0, 2, 3, 4)
    qr_b = q_rope.reshape(B, nb, Q_BLOCK, MLA_HEADS, MLA_ROPE).transpose(1, 0, 2, 3, 4)

    def block(args):
        qn, qr = args
        s = (jnp.einsum('bqhd,bkhd->bhqk', qn, k_nope)
             + jnp.einsum('bqhr,bkr->bhqk', qr, k_r))
        prob = jax.nn.softmax(s.astype(jnp.float32) * scale, axis=-1).astype(v.dtype)
        return jnp.einsum('bhqk,bkhd->bqhd', prob, v)

    o = lax.map(block, (qn_b, qr_b))
    return o.transpose(1, 0, 2, 3, 4).reshape(B, S, MLA_WIDTH)


def gla_chunked(q, k, v, log_a):
    B, S, H, DK = q.shape
    DV = v.shape[-1]
    C = GLA_CHUNK
    N = S // C

    def to_chunks(t):
        return t.astype(jnp.float32).reshape(B, N, C, H, -1).transpose(1, 0, 3, 2, 4)

    qc, kc, vc, gc = to_chunks(q), to_chunks(k), to_chunks(v), to_chunks(log_a)
    b = jnp.cumsum(gc, axis=-2)
    ref = b[..., C // 2 - 1:C // 2, :]
    q_in = qc * jnp.exp(b - ref)
    k_in = kc * jnp.exp(ref - b)
    mask = jnp.tril(jnp.ones((C, C), dtype=bool))
    A = jnp.where(mask, jnp.einsum('nbhid,nbhjd->nbhij', q_in, k_in), 0.0)
    o_intra = jnp.einsum('nbhij,nbhjv->nbhiv', A, vc)

    b_last = b[..., -1:, :]
    q_inter = qc * jnp.exp(b)
    k_state = kc * jnp.exp(b_last - b)
    decay = jnp.exp(b_last[..., 0, :])

    def step(state, inp):
        qi, ki, vi, di = inp
        o = jnp.einsum('bhid,bhdv->bhiv', qi, state)
        state = state * di[..., None] + jnp.einsum('bhjd,bhjv->bhdv', ki, vi)
        return state, o

    s0 = jnp.zeros((B, H, DK, DV), jnp.float32)
    _, o_inter = lax.scan(step, s0, (q_inter, k_state, vc, decay))
    o = o_intra + o_inter
    return o.transpose(1, 0, 3, 2, 4).reshape(B, S, H, DV).astype(v.dtype)


def gla_branch(q, k, v, lr_f, lr_b, w_g_f, b_g_f, w_g_b, b_g_b, out_norm):
    B, S, _ = q.shape
    q = (q * GLA_DK ** -0.5).reshape(B, S, GLA_HEADS, GLA_DK)
    k = k.reshape(B, S, GLA_HEADS, GLA_DK)
    v = v.reshape(B, S, GLA_HEADS, GLA_DV)
    la_f = (jax.nn.log_sigmoid((lr_f @ w_g_f + b_g_f).astype(jnp.float32)) / GLA_TAU)
    la_b = (jax.nn.log_sigmoid((lr_b @ w_g_b + b_g_b).astype(jnp.float32)) / GLA_TAU)
    la_f = la_f.reshape(B, S, GLA_HEADS, GLA_DK)
    la_b = la_b.reshape(B, S, GLA_HEADS, GLA_DK)
    o_f = gla_chunked(q, k, v, la_f)
    flip = lambda t: jnp.flip(t, axis=1)
    o_b = flip(gla_chunked(flip(q), flip(k), flip(v), flip(la_b)))
    o = rmsnorm(o_f + o_b, out_norm)
    return o.reshape(B, S, GLA_WIDTH)


def setup_inputs(seed: int = 0) -> dict:
    key = jax.random.key(seed)
    ks = jax.random.split(key, 24)
    nrm = lambda k, shape, fan_in: jax.random.normal(k, shape, jnp.float32) * fan_in ** -0.5
    gain = lambda k, shape: 1.0 + 0.02 * jax.random.normal(k, shape, jnp.float32)
    L = DEPTH
    x = jax.random.normal(ks[0], (BATCH, SEQ, D_MODEL), jnp.float32)
    p = jax.random.normal(ks[1], (DEPTH, BATCH, SEQ, PLE_DIM), jnp.float32)
    offs = jax.random.randint(ks[2], (BATCH, 1), 0, 1024, dtype=jnp.int32)
    positions = offs + jnp.arange(SEQ, dtype=jnp.int32)[None, :]
    return {
        "x": x,
        "p": p,
        "positions": positions,
        "ln_mix": gain(ks[3], (L, D_MODEL)),
        "w_in": nrm(ks[4], (L, D_MODEL, D_IN), D_MODEL),
        "mla_q_norm": gain(ks[5], (L, MLA_Q_RANK)),
        "w_uq": nrm(ks[6], (L, MLA_Q_RANK, MLA_HEADS * (MLA_NOPE + MLA_ROPE)), MLA_Q_RANK),
        "mla_kv_norm": gain(ks[7], (L, MLA_KV_RANK)),
        "w_ukv": nrm(ks[8], (L, MLA_KV_RANK, MLA_HEADS * (MLA_NOPE + MLA_V)), MLA_KV_RANK),
        "gla_w_gate_fwd": nrm(ks[9], (L, GLA_GATE_RANK, GLA_HEADS * GLA_DK), GLA_GATE_RANK),
        "gla_b_gate_fwd": 0.1 * jax.random.normal(ks[10], (L, GLA_HEADS * GLA_DK), jnp.float32),
        "gla_w_gate_bwd": nrm(ks[11], (L, GLA_GATE_RANK, GLA_HEADS * GLA_DK), GLA_GATE_RANK),
        "gla_b_gate_bwd": 0.1 * jax.random.normal(ks[12], (L, GLA_HEADS * GLA_DK), jnp.float32),
        "gla_out_norm": gain(ks[13], (L, GLA_DV)),
        "w_out": nrm(ks[14], (L, D_MIX, D_MODEL), D_MIX),
        "ple_norm": gain(ks[15], (L, D_MODEL)),
        "w_ple_gate": nrm(ks[16], (L, D_MODEL, D_MODEL), D_MODEL),
        "w_ple_proj": nrm(ks[17], (L, PLE_DIM, D_MODEL), PLE_DIM),
        "final_norm": gain(ks[18], (D_MODEL,)),
    }


def reference(x, p, positions, ln_mix, w_in, mla_q_norm, w_uq, mla_kv_norm, w_ukv,
              gla_w_gate_fwd, gla_b_gate_fwd, gla_w_gate_bwd, gla_b_gate_bwd,
              gla_out_norm, w_out, ple_norm, w_ple_gate, w_ple_proj, final_norm):
    h = x
    for i in range(DEPTH):
        u = rmsnorm(h, ln_mix[i]) @ w_in[i]
        (c_q, c_kv, k_rope, gate_a, gq, gk, gv, lr_f, lr_b, gate_g) = split_cols(u, IN_SPLITS)
        y_mla = mla_branch(c_q, c_kv, k_rope, positions,
                           mla_q_norm[i], w_uq[i], mla_kv_norm[i], w_ukv[i]) * jax.nn.silu(gate_a)
        y_gla = gla_branch(gq, gk, gv, lr_f, lr_b,
                           gla_w_gate_fwd[i], gla_b_gate_fwd[i],
                           gla_w_gate_bwd[i], gla_b_gate_bwd[i],
                           gla_out_norm[i]) * jax.nn.silu(gate_g)
        h = h + jnp.concatenate([y_mla, y_gla], axis=-1) @ w_out[i]
        gate = jax.nn.sigmoid(rmsnorm(h, ple_norm[i]) @ w_ple_gate[i])
        h = h + gate * (p[i] @ w_ple_proj[i])
    return rmsnorm(h, final_norm)
```

```python
import functools
import math

import jax
import jax.numpy as jnp
from jax import lax
from jax.experimental import pallas as pl
from jax.experimental.pallas import tpu as pltpu

EPS = 1e-6
MLA_HEADS = 4
MLA_Q_RANK = 384
MLA_KV_RANK = 256
MLA_NOPE = 128
MLA_ROPE = 64
MLA_V = 128
MLA_WIDTH = MLA_HEADS * MLA_V
ROPE_THETA = 10000.0
GLA_HEADS = 4
GLA_DK = 64
GLA_DV = 128
GLA_WIDTH = GLA_HEADS * GLA_DV
GLA_QK = GLA_HEADS * GLA_DK
GLA_GATE_RANK = 16
GLA_TAU = 16.0

LANES = 128
QK_PAD = 2 * LANES
VMEM_LIMIT = 48 * 1024 * 1024

Q_SCALE = (MLA_NOPE + MLA_ROPE) ** -0.5 * math.log2(math.e)

F32 = jnp.float32
BF16 = jnp.bfloat16


def _tiles(batch, seq):
    tokens = batch * seq
    row_tile = math.gcd(tokens, 512)
    return dict(
        row_tile=row_tile,
        q_tile=math.gcd(seq, 256),
        kv_tile=math.gcd(seq, 512),
        gla_chunk=math.gcd(seq, 128),
        rope_rows=math.gcd(tokens // 4, 1024),
    )


def _dot(a, b):
    return jnp.dot(a, b, preferred_element_type=F32)


def _dot_nt(a, b):
    return lax.dot_general(a, b, (((1,), (1,)), ((), ())), preferred_element_type=F32)


def _dot_tn(a, b):
    return lax.dot_general(a, b, (((0,), (0,)), ((), ())), preferred_element_type=F32)


def _rms(x, g):
    return x * lax.rsqrt(jnp.mean(x * x, axis=-1, keepdims=True) + EPS) * g


def _silu(x):
    return x * jax.nn.sigmoid(x)


def _log_sigmoid(x):
    return jnp.minimum(x, 0.0) - jnp.log1p(jnp.exp(-jnp.abs(x)))


def _rope_table_kernel(pos_ref, inv_ref, cos_ref, sin_ref):
    ang = pos_ref[...] * inv_ref[...]
    cos_ref[...] = jnp.cos(ang)
    sin_ref[...] = jnp.sin(ang)


def _rope_tables(positions, rows):
    half = MLA_ROPE // 2
    tokens = positions.size
    inv = ROPE_THETA ** (-jnp.arange(half, dtype=F32) / half)
    pos = jnp.repeat(positions.reshape(tokens // 4, 4).astype(F32), half, axis=1)
    inv4 = jnp.tile(inv, 4).reshape(1, LANES)
    n = tokens // 4
    spec = pl.BlockSpec((rows, LANES), lambda i: (i, 0))
    cos, sin = pl.pallas_call(
        _rope_table_kernel,
        grid=(n // rows,),
        in_specs=[spec, pl.BlockSpec((1, LANES), lambda i: (0, 0))],
        out_specs=[spec, spec],
        out_shape=[jax.ShapeDtypeStruct((n, LANES), F32)] * 2,
        name="rope_tables",
    )(pos, inv4)
    cos = cos.reshape(tokens, half)
    sin = sin.reshape(tokens, half)
    tab_qc = jnp.concatenate([cos, cos, cos, cos], axis=1) * Q_SCALE
    tab_qs = jnp.concatenate([-sin, sin, -sin, sin], axis=1) * Q_SCALE
    tab_k = jnp.concatenate([cos, cos, -sin, sin], axis=1)
    return tab_qc, tab_qs, tab_k


_C_CQ = 0
_C_CKV = _C_CQ + MLA_Q_RANK
_C_GA = _C_CKV + MLA_KV_RANK
_C_GQ = _C_GA + MLA_WIDTH
_C_GK = _C_GQ + GLA_QK
_C_GV = _C_GK + GLA_QK
_C_GG = _C_GV + GLA_WIDTH
_C_KR = _C_GG + GLA_WIDTH
_C_LR = _C_KR + LANES
_C_END = _C_LR + LANES


def _front_kernel(h_ref, ln_ref, win_ref, qn_ref, wqn_ref, wqr_ref, wqs_ref, kvn_ref, wk_ref, wv_ref,
                  wg_ref, bg_ref, tqc_ref, tqs_ref, tk_ref,
                  q_ref, k_ref, v_ref, sga_ref, gq_ref, gk_ref, gv_ref, laf_ref, lab_ref, sgg_ref):
    xn = _rms(h_ref[...], ln_ref[...]).astype(BF16)

    def proj(c0, c1):
        return _dot(xn, win_ref[:, c0:c1])

    cqn = _rms(proj(_C_CQ, _C_CKV), qn_ref[...]).astype(BF16)
    q_nope = _dot(cqn, wqn_ref[...]) * Q_SCALE
    q_rope = _dot(cqn, wqr_ref[...])
    q_rsw = _dot(cqn, wqs_ref[...])
    tqc = tqc_ref[...]
    tqs = tqs_ref[...]
    for hd in range(MLA_HEADS):
        g = slice(hd * LANES, (hd + 1) * LANES)
        q_ref[:, hd * QK_PAD:hd * QK_PAD + LANES] = q_nope[:, g].astype(BF16)
        q_ref[:, hd * QK_PAD + LANES:(hd + 1) * QK_PAD] = (q_rope[:, g] * tqc + q_rsw[:, g] * tqs).astype(BF16)

    ckvn = _rms(proj(_C_CKV, _C_GA), kvn_ref[...]).astype(BF16)
    k_nope = _dot(ckvn, wk_ref[...])
    v_ref[...] = _dot(ckvn, wv_ref[...]).astype(BF16)
    kr = proj(_C_KR, _C_LR) * tk_ref[...]
    kr = (kr + pltpu.roll(kr, MLA_ROPE, axis=1)).astype(BF16)
    for hd in range(MLA_HEADS):
        k_ref[:, hd * QK_PAD:hd * QK_PAD + LANES] = k_nope[:, hd * LANES:(hd + 1) * LANES].astype(BF16)
        k_ref[:, hd * QK_PAD + LANES:(hd + 1) * QK_PAD] = kr

    sga_ref[...] = _silu(proj(_C_GA, _C_GQ)).astype(BF16)
    sgg_ref[...] = _silu(proj(_C_GG, _C_KR)).astype(BF16)

    gq_ref[...] = (proj(_C_GQ, _C_GK) * GLA_DK ** -0.5).astype(BF16)
    gk_ref[...] = proj(_C_GK, _C_GV).astype(BF16)
    gv_ref[...] = proj(_C_GV, _C_GG).astype(BF16)
    lr = proj(_C_LR, _C_END).astype(BF16)
    la = _log_sigmoid(_dot(lr, wg_ref[...]) + bg_ref[...]) / GLA_TAU
    laf_ref[...] = la[:, :GLA_QK]
    lab_ref[...] = la[:, GLA_QK:]


def _front_weights(w_in, w_uq, w_ukv, wgf, bgf, wgb, bgb):
    d = w_in.shape[0]
    c = 0
    parts = {}
    for name, size in (("cq", MLA_Q_RANK), ("ckv", MLA_KV_RANK), ("kr", MLA_ROPE), ("ga", MLA_WIDTH),
                       ("gq", GLA_QK), ("gk", GLA_QK), ("gv", GLA_WIDTH), ("lrf", GLA_GATE_RANK),
                       ("lrb", GLA_GATE_RANK), ("gg", GLA_WIDTH)):
        parts[name] = w_in[:, c:c + size]
        c += size
    half = MLA_ROPE // 2
    swap = lambda w: jnp.concatenate([w[..., half:], w[..., :half]], axis=-1)
    zeros = lambda n: jnp.zeros((d, n), w_in.dtype)
    win = jnp.concatenate([
        parts["cq"], parts["ckv"], parts["ga"], parts["gq"], parts["gk"], parts["gv"], parts["gg"],
        parts["kr"], swap(parts["kr"]),
        parts["lrf"], parts["lrb"], zeros(LANES - 2 * GLA_GATE_RANK)], axis=1).astype(BF16)

    wq = w_uq.reshape(MLA_Q_RANK, MLA_HEADS, MLA_NOPE + MLA_ROPE)
    wqn = wq[:, :, :MLA_NOPE].reshape(MLA_Q_RANK, MLA_HEADS * MLA_NOPE).astype(BF16)
    wr = wq[:, :, MLA_NOPE:]
    pad = jnp.zeros((MLA_Q_RANK, MLA_HEADS, LANES - MLA_ROPE), w_uq.dtype)
    wqr = jnp.concatenate([wr, pad], axis=-1).reshape(MLA_Q_RANK, MLA_HEADS * LANES).astype(BF16)
    wqs = jnp.concatenate([swap(wr), pad], axis=-1).reshape(MLA_Q_RANK, MLA_HEADS * LANES).astype(BF16)

    wkv = w_ukv.reshape(MLA_KV_RANK, MLA_HEADS, MLA_NOPE + MLA_V)
    wk = wkv[:, :, :MLA_NOPE].reshape(MLA_KV_RANK, MLA_HEADS * MLA_NOPE).astype(BF16)
    wv = wkv[:, :, MLA_NOPE:].reshape(MLA_KV_RANK, MLA_HEADS * MLA_V).astype(BF16)

    wg = jnp.zeros((LANES, 2 * GLA_QK), F32)
    wg = wg.at[:GLA_GATE_RANK, :GLA_QK].set(wgf)
    wg = wg.at[GLA_GATE_RANK:2 * GLA_GATE_RANK, GLA_QK:].set(wgb)
    bg = jnp.concatenate([bgf, bgb]).reshape(1, 2 * GLA_QK)
    return win, wqn, wqr, wqs, wk, wv, wg.astype(BF16), bg


def _front(h, ln, weights, qn, kvn, tabs, tm):
    tokens, d = h.shape
    win, wqn, wqr, wqs, wk, wv, wg, bg = weights
    tqc, tqs, tk = tabs
    row = lambda n: pl.BlockSpec((tm, n), lambda i: (i, 0))
    full = lambda a: pl.BlockSpec(a.shape, lambda i: (0, 0))
    ln = ln.reshape(1, -1)
    qn = qn.reshape(1, -1)
    kvn = kvn.reshape(1, -1)
    outs = [
        (MLA_HEADS * QK_PAD, BF16),
        (MLA_HEADS * QK_PAD, BF16),
        (MLA_WIDTH, BF16),
        (MLA_WIDTH, BF16),
        (GLA_QK, BF16),
        (GLA_QK, BF16),
        (GLA_WIDTH, BF16),
        (GLA_QK, F32),
        (GLA_QK, F32),
        (GLA_WIDTH, BF16),
    ]
    return pl.pallas_call(
        _front_kernel,
        grid=(tokens // tm,),
        in_specs=[row(d), full(ln), full(win), full(qn), full(wqn), full(wqr), full(wqs), full(kvn),
                  full(wk), full(wv), full(wg), full(bg), row(LANES), row(LANES), row(LANES)],
        out_specs=[row(n) for n, _ in outs],
        out_shape=[jax.ShapeDtypeStruct((tokens, n), dt) for n, dt in outs],
        compiler_params=pltpu.CompilerParams(dimension_semantics=("arbitrary",), vmem_limit_bytes=VMEM_LIMIT),
        name="front",
    )(h, ln, win, qn, wqn, wqr, wqs, kvn, wk, wv, wg, bg, tqc, tqs, tk)


def _attn_kernel(q_ref, k_ref, v_ref, g_ref, o_ref, *, kv_tile):
    q = q_ref[0]
    tq = q.shape[0]
    n_kv = k_ref.shape[1] // kv_tile

    def step(j, carry):
        m, l, acc = carry
        ks = k_ref[0, pl.ds(pl.multiple_of(j * kv_tile, kv_tile), kv_tile), :]
        vs = v_ref[0, pl.ds(pl.multiple_of(j * kv_tile, kv_tile), kv_tile), :]
        s = _dot_nt(q, ks)
        m_new = jnp.maximum(m, jnp.max(s, axis=-1, keepdims=True))
        alpha = jnp.exp2(m - m_new)
        p = jnp.exp2(s - m_new)
        l = alpha * l + jnp.sum(p, axis=-1, keepdims=True)
        acc = alpha * acc + _dot(p.astype(BF16), vs)
        return m_new, l, acc

    m0 = jnp.full((tq, 1), -jnp.inf, F32)
    l0 = jnp.zeros((tq, 1), F32)
    a0 = jnp.zeros((tq, MLA_V), F32)
    _, l, acc = lax.fori_loop(0, n_kv, step, (m0, l0, a0))
    o_ref[0] = (acc / l * g_ref[0].astype(F32)).astype(o_ref.dtype)


def _attention(q, k, v, sgate, tq, kv_tile):
    b, s, _ = q.shape
    return pl.pallas_call(
        functools.partial(_attn_kernel, kv_tile=kv_tile),
        grid=(b, MLA_HEADS, s // tq),
        in_specs=[
            pl.BlockSpec((1, tq, QK_PAD), lambda bi, hi, qi: (bi, qi, hi)),
            pl.BlockSpec((1, s, QK_PAD), lambda bi, hi, qi: (bi, 0, hi)),
            pl.BlockSpec((1, s, MLA_V), lambda bi, hi, qi: (bi, 0, hi)),
            pl.BlockSpec((1, tq, MLA_V), lambda bi, hi, qi: (bi, qi, hi)),
        ],
        out_specs=pl.BlockSpec((1, tq, MLA_V), lambda bi, hi, qi: (bi, qi, hi)),
        out_shape=jax.ShapeDtypeStruct((b, s, MLA_WIDTH), BF16),
        compiler_params=pltpu.CompilerParams(
            dimension_semantics=("arbitrary", "arbitrary", "arbitrary"), vmem_limit_bytes=VMEM_LIMIT),
        name="mla_attention",
    )(q, k, v, sgate)


def _split3(x):
    hi = x.astype(BF16)
    r = x - hi.astype(F32)
    mid = r.astype(BF16)
    lo = (r - mid.astype(F32)).astype(BF16)
    return hi, mid, lo


def _gla_direction(q_ref, k_ref, v_ref, la_ref, o_ref, state_ref, forward):
    c = q_ref.shape[1]
    q = q_ref[0].astype(F32)
    k = k_ref[0].astype(F32)
    la = la_ref[0]
    row = lax.broadcasted_iota(jnp.int32, (c, c), 0)
    col = lax.broadcasted_iota(jnp.int32, (c, c), 1)
    tri = (col <= row) if forward else (col >= row)
    tri_b = jnp.where(tri, 1.0, 0.0).astype(BF16)
    hi, mid, lo = _split3(la)
    b = _dot(tri_b, hi) + _dot(tri_b, mid) + _dot(tri_b, lo)
    if forward:
        b_ref = b[c // 2 - 1:c // 2, :]
        b_end = b[c - 1:c, :]
    else:
        b_ref = b[c // 2:c // 2 + 1, :]
        b_end = b[0:1, :]
    q_in = (q * jnp.exp(b - b_ref)).astype(BF16)
    k_in = (k * jnp.exp(b_ref - b)).astype(BF16)
    q_st = (q * jnp.exp(b)).astype(BF16)
    k_st = (k * jnp.exp(b_end - b)).astype(BF16)
    decay = jnp.exp(b_end)
    lane_head = lax.broadcasted_iota(jnp.int32, (c, LANES), 1) // GLA_DK
    zero = jnp.zeros((c, LANES), BF16)
    for hd in range(GLA_HEADS):
        pair = slice((hd // 2) * LANES, (hd // 2 + 1) * LANES)
        mine = lane_head == (hd % 2)
        vh = v_ref[0, :, hd * GLA_DV:(hd + 1) * GLA_DV]
        a = _dot_nt(q_in[:, pair], jnp.where(mine, k_in[:, pair], zero))
        a = jnp.where(tri, a, 0.0).astype(BF16)
        st = state_ref[hd]
        o = _dot(a, vh) + _dot_nt(q_st[:, pair], st.astype(BF16))
        o_ref[0, :, hd * GLA_DV:(hd + 1) * GLA_DV] = o.astype(o_ref.dtype)
        state_ref[hd] = st * decay[:, pair] + _dot_tn(vh, jnp.where(mine, k_st[:, pair], zero))


def _gla_kernel(qf_ref, kf_ref, vf_ref, laf_ref, qb_ref, kb_ref, vb_ref, lab_ref,
                of_ref, ob_ref, sf_ref, sb_ref):
    @pl.when(pl.program_id(1) == 0)
    def _():
        sf_ref[...] = jnp.zeros_like(sf_ref)
        sb_ref[...] = jnp.zeros_like(sb_ref)

    _gla_direction(qf_ref, kf_ref, vf_ref, laf_ref, of_ref, sf_ref, True)
    _gla_direction(qb_ref, kb_ref, vb_ref, lab_ref, ob_ref, sb_ref, False)


def _gla(gq, gk, gv, la_f, la_b, chunk):
    b, s, _ = gq.shape
    n = s // chunk
    fwd = lambda w: pl.BlockSpec((1, chunk, w), lambda bi, ci: (bi, ci, 0))
    bwd = lambda w: pl.BlockSpec((1, chunk, w), lambda bi, ci: (bi, n - 1 - ci, 0))
    out = jax.ShapeDtypeStruct((b, s, GLA_WIDTH), BF16)
    state = pltpu.VMEM((GLA_HEADS, GLA_DV, LANES), F32)
    return pl.pallas_call(
        _gla_kernel,
        grid=(b, n),
        in_specs=[fwd(GLA_QK), fwd(GLA_QK), fwd(GLA_WIDTH), fwd(GLA_QK),
                  bwd(GLA_QK), bwd(GLA_QK), bwd(GLA_WIDTH), bwd(GLA_QK)],
        out_specs=[fwd(GLA_WIDTH), bwd(GLA_WIDTH)],
        out_shape=[out, out],
        scratch_shapes=[state, state],
        compiler_params=pltpu.CompilerParams(
            dimension_semantics=("arbitrary", "arbitrary"), vmem_limit_bytes=VMEM_LIMIT),
        name="gla_scan",
    )(gq, gk, gv, la_f, gq, gk, gv, la_b)


def _back_kernel(h_ref, ym_ref, of_ref, ob_ref, sgg_ref, on_ref, wout_ref, pn_ref, wpg_ref, p_ref, wpp_ref,
                 fn_ref, o_ref, *, last):
    og = of_ref[...].astype(F32) + ob_ref[...].astype(F32)
    sgg = sgg_ref[...].astype(F32)
    on = on_ref[...]
    y = _dot(ym_ref[...], wout_ref[:MLA_WIDTH, :])
    for hd in range(GLA_HEADS):
        g = slice(hd * GLA_DV, (hd + 1) * GLA_DV)
        yg = (_rms(og[:, g], on) * sgg[:, g]).astype(BF16)
        y = y + _dot(yg, wout_ref[MLA_WIDTH + hd * GLA_DV:MLA_WIDTH + (hd + 1) * GLA_DV, :])
    h1 = h_ref[...] + y
    gate = jax.nn.sigmoid(_dot(_rms(h1, pn_ref[...]).astype(BF16), wpg_ref[...]))
    h2 = h1 + gate * _dot(p_ref[...].astype(BF16), wpp_ref[...])
    o_ref[...] = _rms(h2, fn_ref[...]) if last else h2


def _back(h, y_mla, o_f, o_b, sgg, out_norm, w_out, ple_norm, w_pg, p, w_pp, final_norm, tm, last):
    tokens, d = h.shape
    row = lambda n: pl.BlockSpec((tm, n), lambda i: (i, 0))
    full = lambda a: pl.BlockSpec(a.shape, lambda i: (0, 0))
    on = out_norm.reshape(1, -1)
    pn = ple_norm.reshape(1, -1)
    fn = final_norm.reshape(1, -1)
    w_out = w_out.astype(BF16)
    w_pg = w_pg.astype(BF16)
    w_pp = w_pp.astype(BF16)
    return pl.pallas_call(
        functools.partial(_back_kernel, last=last),
        grid=(tokens // tm,),
        in_specs=[row(d), row(MLA_WIDTH), row(GLA_WIDTH), row(GLA_WIDTH), row(GLA_WIDTH), full(on),
                  full(w_out), full(pn), full(w_pg), row(p.shape[1]), full(w_pp), full(fn)],
        out_specs=row(d),
        out_shape=jax.ShapeDtypeStruct((tokens, d), F32),
        compiler_params=pltpu.CompilerParams(dimension_semantics=("arbitrary",), vmem_limit_bytes=VMEM_LIMIT),
        name="back",
    )(h, y_mla, o_f, o_b, sgg, on, w_out, pn, w_pg, p, w_pp, fn)


def kernel(x, p, positions, ln_mix, w_in, mla_q_norm, w_uq, mla_kv_norm, w_ukv, gla_w_gate_fwd, gla_b_gate_fwd,
           gla_w_gate_bwd, gla_b_gate_bwd, gla_out_norm, w_out, ple_norm, w_ple_gate, w_ple_proj, final_norm):
    batch, seq, d = x.shape
    depth = w_in.shape[0]
    tokens = batch * seq
    t = _tiles(batch, seq)
    tabs = _rope_tables(positions, t["rope_rows"])
    h = x.reshape(tokens, d)
    seq3 = lambda a: a.reshape(batch, seq, a.shape[-1])
    for i in range(depth):
        weights = _front_weights(w_in[i], w_uq[i], w_ukv[i], gla_w_gate_fwd[i], gla_b_gate_fwd[i],
                                 gla_w_gate_bwd[i], gla_b_gate_bwd[i])
        q, k, v, sga, gq, gk, gv, la_f, la_b, sgg = _front(
            h, ln_mix[i], weights, mla_q_norm[i], mla_kv_norm[i], tabs, t["row_tile"])
        y_mla = _attention(seq3(q), seq3(k), seq3(v), seq3(sga), t["q_tile"], t["kv_tile"])
        o_f, o_b = _gla(seq3(gq), seq3(gk), seq3(gv), seq3(la_f), seq3(la_b), t["gla_chunk"])
        h = _back(h, y_mla.reshape(tokens, -1), o_f.reshape(tokens, -1), o_b.reshape(tokens, -1), sgg,
                  gla_out_norm[i], w_out[i], ple_norm[i], w_ple_gate[i], p[i].reshape(tokens, -1),
                  w_ple_proj[i], final_norm, t["row_tile"], last=(i == depth - 1))
    return h.reshape(batch, seq, d)
```

```python
import functools
import math

import jax
import jax.numpy as jnp
from jax import lax
from jax.experimental import pallas as pl
from jax.experimental.pallas import tpu as pltpu

EPS = 1e-6
MLA_HEADS = 4
MLA_Q_RANK = 384
MLA_KV_RANK = 256
MLA_NOPE = 128
MLA_ROPE = 64
MLA_V = 128
MLA_WIDTH = MLA_HEADS * MLA_V
ROPE_THETA = 10000.0
GLA_HEADS = 4
GLA_DK = 64
GLA_DV = 128
GLA_WIDTH = GLA_HEADS * GLA_DV
GLA_QK = GLA_HEADS * GLA_DK
GLA_GATE_RANK = 16
GLA_TAU = 16.0

LANES = 128
QK_PAD = 2 * LANES
VMEM_LIMIT = 48 * 1024 * 1024

Q_SCALE = (MLA_NOPE + MLA_ROPE) ** -0.5 * math.log2(math.e)

F32 = jnp.float32
BF16 = jnp.bfloat16


def _tiles(batch, seq):
    tokens = batch * seq
    row_tile = math.gcd(tokens, 512)
    return dict(
        row_tile=row_tile,
        q_tile=math.gcd(seq, 512),
        kv_tile=math.gcd(seq, 1024),
        gla_chunk=math.gcd(seq, 128),
        rope_rows=math.gcd(tokens // 4, 1024),
    )


def _dot(a, b):
    return jnp.dot(a, b, preferred_element_type=F32)


def _dot_nt(a, b):
    return lax.dot_general(a, b, (((1,), (1,)), ((), ())), preferred_element_type=F32)


def _dot_tn(a, b):
    return lax.dot_general(a, b, (((0,), (0,)), ((), ())), preferred_element_type=F32)


def _rms(x, g):
    return x * lax.rsqrt(jnp.mean(x * x, axis=-1, keepdims=True) + EPS) * g


def _silu(x):
    return x * jax.nn.sigmoid(x)


def _log_sigmoid(x):
    return jnp.minimum(x, 0.0) - jnp.log1p(jnp.exp(-jnp.abs(x)))


def _rope_table_kernel(pos_ref, inv_ref, cos_ref, sin_ref):
    ang = pos_ref[...] * inv_ref[...]
    cos_ref[...] = jnp.cos(ang)
    sin_ref[...] = jnp.sin(ang)


def _rope_tables(positions, rows):
    half = MLA_ROPE // 2
    tokens = positions.size
    inv = ROPE_THETA ** (-jnp.arange(half, dtype=F32) / half)
    pos = jnp.repeat(positions.reshape(tokens // 4, 4).astype(F32), half, axis=1)
    inv4 = jnp.tile(inv, 4).reshape(1, LANES)
    n = tokens // 4
    spec = pl.BlockSpec((rows, LANES), lambda i: (i, 0))
    cos, sin = pl.pallas_call(
        _rope_table_kernel,
        grid=(n // rows,),
        in_specs=[spec, pl.BlockSpec((1, LANES), lambda i: (0, 0))],
        out_specs=[spec, spec],
        out_shape=[jax.ShapeDtypeStruct((n, LANES), F32)] * 2,
        name="rope_tables",
    )(pos, inv4)
    cos = cos.reshape(tokens, half)
    sin = sin.reshape(tokens, half)
    tab_qc = jnp.concatenate([cos, cos, cos, cos], axis=1) * Q_SCALE
    tab_qs = jnp.concatenate([-sin, sin, -sin, sin], axis=1) * Q_SCALE
    tab_k = jnp.concatenate([cos, cos, -sin, sin], axis=1)
    return tab_qc, tab_qs, tab_k


_C_CQ = 0
_C_CKV = _C_CQ + MLA_Q_RANK
_C_GA = _C_CKV + MLA_KV_RANK
_C_GQ = _C_GA + MLA_WIDTH
_C_GK = _C_GQ + GLA_QK
_C_GV = _C_GK + GLA_QK
_C_GG = _C_GV + GLA_WIDTH
_C_KR = _C_GG + GLA_WIDTH
_C_LR = _C_KR + LANES
_C_END = _C_LR + LANES


def _front_kernel(h_ref, ln_ref, win_ref, qn_ref, wqn_ref, wqr_ref, wqs_ref, kvn_ref, wk_ref, wv_ref,
                  wg_ref, bg_ref, tqc_ref, tqs_ref, tk_ref,
                  q_ref, k_ref, v_ref, sga_ref, gq_ref, gk_ref, gv_ref, laf_ref, lab_ref, sgg_ref):
    xn = _rms(h_ref[...], ln_ref[...]).astype(BF16)

    def proj(c0, c1):
        return _dot(xn, win_ref[:, c0:c1])

    cqn = _rms(proj(_C_CQ, _C_CKV), qn_ref[...]).astype(BF16)
    q_nope = _dot(cqn, wqn_ref[...]) * Q_SCALE
    q_rope = _dot(cqn, wqr_ref[...])
    q_rsw = _dot(cqn, wqs_ref[...])
    tqc = tqc_ref[...]
    tqs = tqs_ref[...]
    for hd in range(MLA_HEADS):
        g = slice(hd * LANES, (hd + 1) * LANES)
        q_ref[:, hd * QK_PAD:hd * QK_PAD + LANES] = q_nope[:, g].astype(BF16)
        q_ref[:, hd * QK_PAD + LANES:(hd + 1) * QK_PAD] = (q_rope[:, g] * tqc + q_rsw[:, g] * tqs).astype(BF16)

    ckvn = _rms(proj(_C_CKV, _C_GA), kvn_ref[...]).astype(BF16)
    k_nope = _dot(ckvn, wk_ref[...])
    v_ref[...] = _dot(ckvn, wv_ref[...]).astype(BF16)
    kr = proj(_C_KR, _C_LR) * tk_ref[...]
    kr = (kr + pltpu.roll(kr, MLA_ROPE, axis=1)).astype(BF16)
    for hd in range(MLA_HEADS):
        k_ref[:, hd * QK_PAD:hd * QK_PAD + LANES] = k_nope[:, hd * LANES:(hd + 1) * LANES].astype(BF16)
        k_ref[:, hd * QK_PAD + LANES:(hd + 1) * QK_PAD] = kr

    sga_ref[...] = _silu(proj(_C_GA, _C_GQ)).astype(BF16)
    sgg_ref[...] = _silu(proj(_C_GG, _C_KR)).astype(BF16)

    gq_ref[...] = (proj(_C_GQ, _C_GK) * GLA_DK ** -0.5).astype(BF16)
    gk_ref[...] = proj(_C_GK, _C_GV).astype(BF16)
    gv_ref[...] = proj(_C_GV, _C_GG).astype(BF16)
    lr = proj(_C_LR, _C_END).astype(BF16)
    la = _log_sigmoid(_dot(lr, wg_ref[...]) + bg_ref[...]) / GLA_TAU
    laf_ref[...] = la[:, :GLA_QK]
    lab_ref[...] = la[:, GLA_QK:]


def _front_weights(w_in, w_uq, w_ukv, wgf, bgf, wgb, bgb):
    d = w_in.shape[0]
    c = 0
    parts = {}
    for name, size in (("cq", MLA_Q_RANK), ("ckv", MLA_KV_RANK), ("kr", MLA_ROPE), ("ga", MLA_WIDTH),
                       ("gq", GLA_QK), ("gk", GLA_QK), ("gv", GLA_WIDTH), ("lrf", GLA_GATE_RANK),
                       ("lrb", GLA_GATE_RANK), ("gg", GLA_WIDTH)):
        parts[name] = w_in[:, c:c + size]
        c += size
    half = MLA_ROPE // 2
    swap = lambda w: jnp.concatenate([w[..., half:], w[..., :half]], axis=-1)
    zeros = lambda n: jnp.zeros((d, n), w_in.dtype)
    win = jnp.concatenate([
        parts["cq"], parts["ckv"], parts["ga"], parts["gq"], parts["gk"], parts["gv"], parts["gg"],
        parts["kr"], swap(parts["kr"]),
        parts["lrf"], parts["lrb"], zeros(LANES - 2 * GLA_GATE_RANK)], axis=1).astype(BF16)

    wq = w_uq.reshape(MLA_Q_RANK, MLA_HEADS, MLA_NOPE + MLA_ROPE)
    wqn = wq[:, :, :MLA_NOPE].reshape(MLA_Q_RANK, MLA_HEADS * MLA_NOPE).astype(BF16)
    wr = wq[:, :, MLA_NOPE:]
    pad = jnp.zeros((MLA_Q_RANK, MLA_HEADS, LANES - MLA_ROPE), w_uq.dtype)
    wqr = jnp.concatenate([wr, pad], axis=-1).reshape(MLA_Q_RANK, MLA_HEADS * LANES).astype(BF16)
    wqs = jnp.concatenate([swap(wr), pad], axis=-1).reshape(MLA_Q_RANK, MLA_HEADS * LANES).astype(BF16)

    wkv = w_ukv.reshape(MLA_KV_RANK, MLA_HEADS, MLA_NOPE + MLA_V)
    wk = wkv[:, :, :MLA_NOPE].reshape(MLA_KV_RANK, MLA_HEADS * MLA_NOPE).astype(BF16)
    wv = wkv[:, :, MLA_NOPE:].reshape(MLA_KV_RANK, MLA_HEADS * MLA_V).astype(BF16)

    wg = jnp.zeros((LANES, 2 * GLA_QK), F32)
    wg = wg.at[:GLA_GATE_RANK, :GLA_QK].set(wgf)
    wg = wg.at[GLA_GATE_RANK:2 * GLA_GATE_RANK, GLA_QK:].set(wgb)
    bg = jnp.concatenate([bgf, bgb]).reshape(1, 2 * GLA_QK)
    return win, wqn, wqr, wqs, wk, wv, wg.astype(BF16), bg


def _front(h, ln, weights, qn, kvn, tabs, tm):
    tokens, d = h.shape
    win, wqn, wqr, wqs, wk, wv, wg, bg = weights
    tqc, tqs, tk = tabs
    row = lambda n: pl.BlockSpec((tm, n), lambda i: (i, 0))
    full = lambda a: pl.BlockSpec(a.shape, lambda i: (0, 0))
    ln = ln.reshape(1, -1)
    qn = qn.reshape(1, -1)
    kvn = kvn.reshape(1, -1)
    outs = [
        (MLA_HEADS * QK_PAD, BF16),
        (MLA_HEADS * QK_PAD, BF16),
        (MLA_WIDTH, BF16),
        (MLA_WIDTH, BF16),
        (GLA_QK, BF16),
        (GLA_QK, BF16),
        (GLA_WIDTH, BF16),
        (GLA_QK, F32),
        (GLA_QK, F32),
        (GLA_WIDTH, BF16),
    ]
    return pl.pallas_call(
        _front_kernel,
        grid=(tokens // tm,),
        in_specs=[row(d), full(ln), full(win), full(qn), full(wqn), full(wqr), full(wqs), full(kvn),
                  full(wk), full(wv), full(wg), full(bg), row(LANES), row(LANES), row(LANES)],
        out_specs=[row(n) for n, _ in outs],
        out_shape=[jax.ShapeDtypeStruct((tokens, n), dt) for n, dt in outs],
        compiler_params=pltpu.CompilerParams(dimension_semantics=("arbitrary",), vmem_limit_bytes=VMEM_LIMIT),
        name="front",
    )(h, ln, win, qn, wqn, wqr, wqs, kvn, wk, wv, wg, bg, tqc, tqs, tk)


def _attn_kernel(q_ref, k_ref, v_ref, g_ref, o_ref, s_ref, m_ref, macc_ref, lacc_ref, acc_ref, *, kv_tile):
    t = pl.program_id(0)
    n_kv = k_ref.shape[1] // kv_tile
    groups = kv_tile // LANES
    slot_a = t % 2
    slot_b = 1 - slot_a

    @pl.when(t == 0)
    def _():
        s_ref[...] = jnp.zeros_like(s_ref)
        m_ref[...] = jnp.zeros_like(m_ref)

    tq = q_ref.shape[1]
    macc_ref[...] = jnp.full((tq, LANES), -jnp.inf, F32)
    lacc_ref[...] = jnp.zeros((tq, LANES), F32)
    acc_ref[...] = jnp.zeros((tq, MLA_V), F32)

    @pl.loop(0, n_kv)
    def _(j):
        rows = pl.ds(pl.multiple_of(j * kv_tile, kv_tile), kv_tile)
        s = _dot_nt(q_ref[0], k_ref[0, rows, :])
        s_ref[slot_a, :, rows] = s
        m_acc = macc_ref[...]
        for c in range(groups):
            m_acc = jnp.maximum(m_acc, s[:, c * LANES:(c + 1) * LANES])
        macc_ref[...] = m_acc
        m_b = m_ref[slot_b]
        s_prev = s_ref[slot_b, :, rows]
        l_acc = lacc_ref[...]
        cols = []
        for c in range(groups):
            pc = jnp.exp2(s_prev[:, c * LANES:(c + 1) * LANES] - m_b)
            l_acc = l_acc + pc
            cols.append(pc.astype(BF16))
        lacc_ref[...] = l_acc
        acc_ref[...] += _dot(jnp.concatenate(cols, axis=1), v_ref[0, rows, :])

    m_ref[slot_a] = jnp.broadcast_to(jnp.max(macc_ref[...], axis=-1, keepdims=True), (tq, LANES))
    l = jnp.sum(lacc_ref[...], axis=-1, keepdims=True)
    o_ref[0] = (acc_ref[...] / l * g_ref[0].astype(F32)).astype(o_ref.dtype)


def _attention(q, k, v, sgate, tq, kv_tile):
    b, s, _ = q.shape
    nq = s // tq
    items = b * MLA_HEADS * nq

    def item(t):
        return t // (MLA_HEADS * nq), (t // nq) % MLA_HEADS, t % nq

    def stage_a(t):
        return item(jnp.minimum(t, items - 1))

    def stage_b(t):
        return item(jnp.maximum(t - 1, 0))

    def q_map(t):
        bi, hi, qi = stage_a(t)
        return bi, qi, hi

    def k_map(t):
        bi, hi, _ = stage_a(t)
        return bi, 0, hi

    def v_map(t):
        bi, hi, _ = stage_b(t)
        return bi, 0, hi

    def o_map(t):
        bi, hi, qi = stage_b(t)
        return bi, qi, hi

    return pl.pallas_call(
        functools.partial(_attn_kernel, kv_tile=kv_tile),
        grid=(items + 1,),
        in_specs=[
            pl.BlockSpec((1, tq, QK_PAD), q_map),
            pl.BlockSpec((1, s, QK_PAD), k_map),
            pl.BlockSpec((1, s, MLA_V), v_map),
            pl.BlockSpec((1, tq, MLA_V), o_map),
        ],
        out_specs=pl.BlockSpec((1, tq, MLA_V), o_map),
        out_shape=jax.ShapeDtypeStruct((b, s, MLA_WIDTH), BF16),
        scratch_shapes=[pltpu.VMEM((2, tq, s), F32), pltpu.VMEM((2, tq, LANES), F32),
                        pltpu.VMEM((tq, LANES), F32), pltpu.VMEM((tq, LANES), F32), pltpu.VMEM((tq, MLA_V), F32)],
        compiler_params=pltpu.CompilerParams(dimension_semantics=("arbitrary",), vmem_limit_bytes=VMEM_LIMIT),
        name="mla_attention",
    )(q, k, v, sgate)


def _split3(x):
    hi = x.astype(BF16)
    r = x - hi.astype(F32)
    mid = r.astype(BF16)
    lo = (r - mid.astype(F32)).astype(BF16)
    return hi, mid, lo


def _gla_direction(q_ref, k_ref, v_ref, la_ref, o_ref, state_ref, forward):
    c = q_ref.shape[1]
    q = q_ref[0].astype(F32)
    k = k_ref[0].astype(F32)
    la = la_ref[0]
    row = lax.broadcasted_iota(jnp.int32, (c, c), 0)
    col = lax.broadcasted_iota(jnp.int32, (c, c), 1)
    tri = (col <= row) if forward else (col >= row)
    tri_b = jnp.where(tri, 1.0, 0.0).astype(BF16)
    hi, mid, lo = _split3(la)
    b = _dot(tri_b, hi) + _dot(tri_b, mid) + _dot(tri_b, lo)
    if forward:
        b_ref = b[c // 2 - 1:c // 2, :]
        b_end = b[c - 1:c, :]
    else:
        b_ref = b[c // 2:c // 2 + 1, :]
        b_end = b[0:1, :]
    q_in = (q * jnp.exp(b - b_ref)).astype(BF16)
    k_in = (k * jnp.exp(b_ref - b)).astype(BF16)
    q_st = (q * jnp.exp(b)).astype(BF16)
    k_st = (k * jnp.exp(b_end - b)).astype(BF16)
    decay = jnp.exp(b_end)
    lane_head = lax.broadcasted_iota(jnp.int32, (c, LANES), 1) // GLA_DK
    zero = jnp.zeros((c, LANES), BF16)
    for hd in range(GLA_HEADS):
        pair = slice((hd // 2) * LANES, (hd // 2 + 1) * LANES)
        mine = lane_head == (hd % 2)
        vh = v_ref[0, :, hd * GLA_DV:(hd + 1) * GLA_DV]
        a = _dot_nt(q_in[:, pair], jnp.where(mine, k_in[:, pair], zero))
        a = jnp.where(tri, a, 0.0).astype(BF16)
        st = state_ref[hd]
        o = _dot(a, vh) + _dot_nt(q_st[:, pair], st.astype(BF16))
        o_ref[0, :, hd * GLA_DV:(hd + 1) * GLA_DV] = o.astype(o_ref.dtype)
        state_ref[hd] = st * decay[:, pair] + _dot_tn(vh, jnp.where(mine, k_st[:, pair], zero))


def _gla_kernel(qf_ref, kf_ref, vf_ref, laf_ref, qb_ref, kb_ref, vb_ref, lab_ref,
                of_ref, ob_ref, sf_ref, sb_ref):
    @pl.when(pl.program_id(1) == 0)
    def _():
        sf_ref[...] = jnp.zeros_like(sf_ref)
        sb_ref[...] = jnp.zeros_like(sb_ref)

    _gla_direction(qf_ref, kf_ref, vf_ref, laf_ref, of_ref, sf_ref, True)
    _gla_direction(qb_ref, kb_ref, vb_ref, lab_ref, ob_ref, sb_ref, False)


def _gla(gq, gk, gv, la_f, la_b, chunk):
    b, s, _ = gq.shape
    n = s // chunk
    fwd = lambda w: pl.BlockSpec((1, chunk, w), lambda bi, ci: (bi, ci, 0))
    bwd = lambda w: pl.BlockSpec((1, chunk, w), lambda bi, ci: (bi, n - 1 - ci, 0))
    out = jax.ShapeDtypeStruct((b, s, GLA_WIDTH), BF16)
    state = pltpu.VMEM((GLA_HEADS, GLA_DV, LANES), F32)
    return pl.pallas_call(
        _gla_kernel,
        grid=(b, n),
        in_specs=[fwd(GLA_QK), fwd(GLA_QK), fwd(GLA_WIDTH), fwd(GLA_QK),
                  bwd(GLA_QK), bwd(GLA_QK), bwd(GLA_WIDTH), bwd(GLA_QK)],
        out_specs=[fwd(GLA_WIDTH), bwd(GLA_WIDTH)],
        out_shape=[out, out],
        scratch_shapes=[state, state],
        compiler_params=pltpu.CompilerParams(
            dimension_semantics=("arbitrary", "arbitrary"), vmem_limit_bytes=VMEM_LIMIT),
        name="gla_scan",
    )(gq, gk, gv, la_f, gq, gk, gv, la_b)


def _back_kernel(h_ref, ym_ref, of_ref, ob_ref, sgg_ref, on_ref, wout_ref, pn_ref, wpg_ref, p_ref, wpp_ref,
                 fn_ref, o_ref, *, last):
    og = of_ref[...].astype(F32) + ob_ref[...].astype(F32)
    sgg = sgg_ref[...].astype(F32)
    on = on_ref[...]
    y = _dot(ym_ref[...], wout_ref[:MLA_WIDTH, :])
    for hd in range(GLA_HEADS):
        g = slice(hd * GLA_DV, (hd + 1) * GLA_DV)
        yg = (_rms(og[:, g], on) * sgg[:, g]).astype(BF16)
        y = y + _dot(yg, wout_ref[MLA_WIDTH + hd * GLA_DV:MLA_WIDTH + (hd + 1) * GLA_DV, :])
    h1 = h_ref[...] + y
    gate = jax.nn.sigmoid(_dot(_rms(h1, pn_ref[...]).astype(BF16), wpg_ref[...]))
    h2 = h1 + gate * _dot(p_ref[...].astype(BF16), wpp_ref[...])
    o_ref[...] = _rms(h2, fn_ref[...]) if last else h2


def _back(h, y_mla, o_f, o_b, sgg, out_norm, w_out, ple_norm, w_pg, p, w_pp, final_norm, tm, last):
    tokens, d = h.shape
    row = lambda n: pl.BlockSpec((tm, n), lambda i: (i, 0))
    full = lambda a: pl.BlockSpec(a.shape, lambda i: (0, 0))
    on = out_norm.reshape(1, -1)
    pn = ple_norm.reshape(1, -1)
    fn = final_norm.reshape(1, -1)
    w_out = w_out.astype(BF16)
    w_pg = w_pg.astype(BF16)
    w_pp = w_pp.astype(BF16)
    return pl.pallas_call(
        functools.partial(_back_kernel, last=last),
        grid=(tokens // tm,),
        in_specs=[row(d), row(MLA_WIDTH), row(GLA_WIDTH), row(GLA_WIDTH), row(GLA_WIDTH), full(on),
                  full(w_out), full(pn), full(w_pg), row(p.shape[1]), full(w_pp), full(fn)],
        out_specs=row(d),
        out_shape=jax.ShapeDtypeStruct((tokens, d), F32),
        compiler_params=pltpu.CompilerParams(dimension_semantics=("arbitrary",), vmem_limit_bytes=VMEM_LIMIT),
        name="back",
    )(h, y_mla, o_f, o_b, sgg, on, w_out, pn, w_pg, p, w_pp, fn)


def kernel(x, p, positions, ln_mix, w_in, mla_q_norm, w_uq, mla_kv_norm, w_ukv, gla_w_gate_fwd, gla_b_gate_fwd,
           gla_w_gate_bwd, gla_b_gate_bwd, gla_out_norm, w_out, ple_norm, w_ple_gate, w_ple_proj, final_norm):
    batch, seq, d = x.shape
    depth = w_in.shape[0]
    tokens = batch * seq
    t = _tiles(batch, seq)
    tabs = _rope_tables(positions, t["rope_rows"])
    h = x.reshape(tokens, d)
    seq3 = lambda a: a.reshape(batch, seq, a.shape[-1])
    for i in range(depth):
        weights = _front_weights(w_in[i], w_uq[i], w_ukv[i], gla_w_gate_fwd[i], gla_b_gate_fwd[i],
                                 gla_w_gate_bwd[i], gla_b_gate_bwd[i])
        q, k, v, sga, gq, gk, gv, la_f, la_b, sgg = _front(
            h, ln_mix[i], weights, mla_q_norm[i], mla_kv_norm[i], tabs, t["row_tile"])
        y_mla = _attention(seq3(q), seq3(k), seq3(v), seq3(sga), t["q_tile"], t["kv_tile"])
        o_f, o_b = _gla(seq3(gq), seq3(gk), seq3(gv), seq3(la_f), seq3(la_b), t["gla_chunk"])
        h = _back(h, y_mla.reshape(tokens, -1), o_f.reshape(tokens, -1), o_b.reshape(tokens, -1), sgg,
                  gla_out_norm[i], w_out[i], ple_norm[i], w_ple_gate[i], p[i].reshape(tokens, -1),
                  w_ple_proj[i], final_norm, t["row_tile"], last=(i == depth - 1))
    return h.reshape(batch, seq, d)
```

```python
import functools
import math

import jax
import jax.numpy as jnp
from jax import lax
from jax.experimental import pallas as pl
from jax.experimental.pallas import tpu as pltpu

EPS = 1e-6
MLA_HEADS = 4
MLA_Q_RANK = 384
MLA_KV_RANK = 256
MLA_NOPE = 128
MLA_ROPE = 64
MLA_V = 128
MLA_WIDTH = MLA_HEADS * MLA_V
ROPE_THETA = 10000.0
GLA_HEADS = 4
GLA_DK = 64
GLA_DV = 128
GLA_WIDTH = GLA_HEADS * GLA_DV
GLA_QK = GLA_HEADS * GLA_DK
GLA_GATE_RANK = 16
GLA_TAU = 16.0

LANES = 128
QK_PAD = 2 * LANES
VMEM_LIMIT = 48 * 1024 * 1024

Q_SCALE = (MLA_NOPE + MLA_ROPE) ** -0.5 * math.log2(math.e)

F32 = jnp.float32
BF16 = jnp.bfloat16


def _tiles(batch, seq):
    tokens = batch * seq
    row_tile = math.gcd(tokens, 512)
    return dict(
        row_tile=row_tile,
        q_tile=math.gcd(seq, 512),
        kv_tile=math.gcd(seq, 2048),
        gla_chunk=math.gcd(seq, 256),
        rope_rows=math.gcd(tokens // 4, 1024),
    )


def _dot(a, b):
    return jnp.dot(a, b, preferred_element_type=F32)


def _dot_nt(a, b):
    return lax.dot_general(a, b, (((1,), (1,)), ((), ())), preferred_element_type=F32)


def _dot_tn(a, b):
    return lax.dot_general(a, b, (((0,), (0,)), ((), ())), preferred_element_type=F32)


def _rms(x, g):
    return x * lax.rsqrt(jnp.mean(x * x, axis=-1, keepdims=True) + EPS) * g


def _silu(x):
    return x * jax.nn.sigmoid(x)


def _log_sigmoid(x):
    return jnp.minimum(x, 0.0) - jnp.log1p(jnp.exp(-jnp.abs(x)))


def _rope_table_kernel(pos_ref, inv_ref, cos_ref, sin_ref):
    ang = pos_ref[...] * inv_ref[...]
    cos_ref[...] = jnp.cos(ang)
    sin_ref[...] = jnp.sin(ang)


def _rope_tables(positions, rows):
    half = MLA_ROPE // 2
    tokens = positions.size
    inv = ROPE_THETA ** (-jnp.arange(half, dtype=F32) / half)
    pos = jnp.repeat(positions.reshape(tokens // 4, 4).astype(F32), half, axis=1)
    inv4 = jnp.tile(inv, 4).reshape(1, LANES)
    n = tokens // 4
    spec = pl.BlockSpec((rows, LANES), lambda i: (i, 0))
    cos, sin = pl.pallas_call(
        _rope_table_kernel,
        grid=(n // rows,),
        in_specs=[spec, pl.BlockSpec((1, LANES), lambda i: (0, 0))],
        out_specs=[spec, spec],
        out_shape=[jax.ShapeDtypeStruct((n, LANES), F32)] * 2,
        name="rope_tables",
    )(pos, inv4)
    cos = cos.reshape(tokens, half)
    sin = sin.reshape(tokens, half)
    zero = jnp.zeros_like(cos)
    tab_q = jnp.concatenate([cos, cos, -sin, sin], axis=1) * Q_SCALE
    tab_kc = jnp.concatenate([cos, cos, zero, zero], axis=1)
    tab_ks = jnp.concatenate([-sin, sin, zero, zero], axis=1)
    return tab_q, tab_kc, tab_ks


_C_CQ = 0
_C_KRLR = _C_CQ + MLA_Q_RANK
_C_CKV = _C_KRLR + LANES
_C_GA = _C_CKV + MLA_KV_RANK
_C_GQ = _C_GA + MLA_WIDTH
_C_GK = _C_GQ + GLA_QK
_C_GV = _C_GK + GLA_QK
_C_GG = _C_GV + GLA_WIDTH
_C_END = _C_GG + GLA_WIDTH
_LR_LANE = MLA_ROPE


def _front_kernel(h_ref, ln_ref, win_ref, qn_ref, wqn_ref, wqr_ref, kvn_ref, wk_ref, wv_ref,
                  wg_ref, bg_ref, tq_ref, tkc_ref, tks_ref,
                  q_ref, k_ref, v_ref, sga_ref, gq_ref, gk_ref, gv_ref, laf_ref, lab_ref, sgg_ref):
    xn = _rms(h_ref[...], ln_ref[...]).astype(BF16)

    def proj(c0, c1):
        return _dot(xn, win_ref[:, c0:c1])

    cq_krlr = proj(_C_CQ, _C_CKV)
    krlr = cq_krlr[:, _C_KRLR:_C_CKV]

    cqn = _rms(cq_krlr[:, :_C_KRLR], qn_ref[...]).astype(BF16)
    q_nope = _dot(cqn, wqn_ref[...]) * Q_SCALE
    q_rope = _dot(cqn, wqr_ref[...])
    tq = tq_ref[...]
    for hd in range(MLA_HEADS):
        g = slice(hd * LANES, (hd + 1) * LANES)
        q_ref[:, hd * QK_PAD:hd * QK_PAD + LANES] = q_nope[:, g].astype(BF16)
        qr = q_rope[:, g] * tq
        q_ref[:, hd * QK_PAD + LANES:(hd + 1) * QK_PAD] = (qr + pltpu.roll(qr, MLA_ROPE, axis=1)).astype(BF16)

    ckvn = _rms(proj(_C_CKV, _C_GA), kvn_ref[...]).astype(BF16)
    k_nope = _dot(ckvn, wk_ref[...])
    v_ref[...] = _dot(ckvn, wv_ref[...]).astype(BF16)
    half = MLA_ROPE // 2
    lane = lax.broadcasted_iota(jnp.int32, krlr.shape, 1)
    kr_sw = jnp.where(lane < half, pltpu.roll(krlr, LANES - half, axis=1), pltpu.roll(krlr, half, axis=1))
    kr = (krlr * tkc_ref[...] + kr_sw * tks_ref[...]).astype(BF16)
    for hd in range(MLA_HEADS):
        k_ref[:, hd * QK_PAD:hd * QK_PAD + LANES] = k_nope[:, hd * LANES:(hd + 1) * LANES].astype(BF16)
        k_ref[:, hd * QK_PAD + LANES:(hd + 1) * QK_PAD] = kr

    sga_ref[...] = _silu(proj(_C_GA, _C_GQ)).astype(BF16)
    sgg_ref[...] = _silu(proj(_C_GG, _C_END)).astype(BF16)

    gq_ref[...] = (proj(_C_GQ, _C_GK) * GLA_DK ** -0.5).astype(BF16)
    gk_ref[...] = proj(_C_GK, _C_GV).astype(BF16)
    gv_ref[...] = proj(_C_GV, _C_GG).astype(BF16)
    la = _log_sigmoid(_dot(krlr.astype(BF16), wg_ref[...]) + bg_ref[...]) / GLA_TAU
    laf_ref[...] = la[:, :GLA_QK]
    lab_ref[...] = la[:, GLA_QK:]


def _front_weights(w_in, w_uq, w_ukv, wgf, bgf, wgb, bgb):
    d = w_in.shape[0]
    c = 0
    parts = {}
    for name, size in (("cq", MLA_Q_RANK), ("ckv", MLA_KV_RANK), ("kr", MLA_ROPE), ("ga", MLA_WIDTH),
                       ("gq", GLA_QK), ("gk", GLA_QK), ("gv", GLA_WIDTH), ("lrf", GLA_GATE_RANK),
                       ("lrb", GLA_GATE_RANK), ("gg", GLA_WIDTH)):
        parts[name] = w_in[:, c:c + size]
        c += size
    half = MLA_ROPE // 2
    swap = lambda w: jnp.concatenate([w[..., half:], w[..., :half]], axis=-1)
    zeros = lambda n: jnp.zeros((d, n), w_in.dtype)
    win = jnp.concatenate([
        parts["cq"],
        parts["kr"], parts["lrf"], parts["lrb"], zeros(LANES - MLA_ROPE - 2 * GLA_GATE_RANK),
        parts["ckv"], parts["ga"], parts["gq"], parts["gk"], parts["gv"], parts["gg"]], axis=1).astype(BF16)

    wq = w_uq.reshape(MLA_Q_RANK, MLA_HEADS, MLA_NOPE + MLA_ROPE)
    wqn = wq[:, :, :MLA_NOPE].reshape(MLA_Q_RANK, MLA_HEADS * MLA_NOPE).astype(BF16)
    wr = wq[:, :, MLA_NOPE:]
    wqr = jnp.concatenate([wr, swap(wr)], axis=-1).reshape(MLA_Q_RANK, MLA_HEADS * LANES).astype(BF16)

    wkv = w_ukv.reshape(MLA_KV_RANK, MLA_HEADS, MLA_NOPE + MLA_V)
    wk = wkv[:, :, :MLA_NOPE].reshape(MLA_KV_RANK, MLA_HEADS * MLA_NOPE).astype(BF16)
    wv = wkv[:, :, MLA_NOPE:].reshape(MLA_KV_RANK, MLA_HEADS * MLA_V).astype(BF16)

    wg = jnp.zeros((LANES, 2 * GLA_QK), F32)
    wg = wg.at[_LR_LANE:_LR_LANE + GLA_GATE_RANK, :GLA_QK].set(wgf)
    wg = wg.at[_LR_LANE + GLA_GATE_RANK:_LR_LANE + 2 * GLA_GATE_RANK, GLA_QK:].set(wgb)
    bg = jnp.concatenate([bgf, bgb]).reshape(1, 2 * GLA_QK)
    return win, wqn, wqr, wk, wv, wg.astype(BF16), bg


def _front(h, ln, weights, qn, kvn, tabs, tm):
    tokens, d = h.shape
    win, wqn, wqr, wk, wv, wg, bg = weights
    tq, tkc, tks = tabs
    row = lambda n: pl.BlockSpec((tm, n), lambda i: (i, 0))
    full = lambda a: pl.BlockSpec(a.shape, lambda i: (0, 0))
    ln = ln.reshape(1, -1)
    qn = qn.reshape(1, -1)
    kvn = kvn.reshape(1, -1)
    outs = [
        (MLA_HEADS * QK_PAD, BF16),
        (MLA_HEADS * QK_PAD, BF16),
        (MLA_WIDTH, BF16),
        (MLA_WIDTH, BF16),
        (GLA_QK, BF16),
        (GLA_QK, BF16),
        (GLA_WIDTH, BF16),
        (GLA_QK, F32),
        (GLA_QK, F32),
        (GLA_WIDTH, BF16),
    ]
    return pl.pallas_call(
        _front_kernel,
        grid=(tokens // tm,),
        in_specs=[row(d), full(ln), full(win), full(qn), full(wqn), full(wqr), full(kvn),
                  full(wk), full(wv), full(wg), full(bg), row(LANES), row(LANES), row(LANES)],
        out_specs=[row(n) for n, _ in outs],
        out_shape=[jax.ShapeDtypeStruct((tokens, n), dt) for n, dt in outs],
        compiler_params=pltpu.CompilerParams(dimension_semantics=("arbitrary",), vmem_limit_bytes=VMEM_LIMIT),
        name="front",
    )(h, ln, win, qn, wqn, wqr, kvn, wk, wv, wg, bg, tq, tkc, tks)


def _attn_kernel(q_ref, k_ref, v_ref, g_ref, o_ref, s_ref, m_ref, macc_ref, lacc_ref, acc_ref, *, kv_tile):
    t = pl.program_id(0)
    n_kv = k_ref.shape[1] // kv_tile
    groups = kv_tile // LANES
    slot_a = t % 2
    slot_b = 1 - slot_a

    @pl.when(t == 0)
    def _():
        s_ref[...] = jnp.zeros_like(s_ref)
        m_ref[...] = jnp.zeros_like(m_ref)

    tq = q_ref.shape[1]
    macc_ref[...] = jnp.full((tq, LANES), -jnp.inf, F32)
    lacc_ref[...] = jnp.zeros((tq, LANES), F32)
    acc_ref[...] = jnp.zeros((tq, MLA_V), F32)

    @pl.loop(0, n_kv)
    def _(j):
        rows = pl.ds(pl.multiple_of(j * kv_tile, kv_tile), kv_tile)
        s = _dot_nt(q_ref[0], k_ref[0, rows, :])
        s_ref[slot_a, :, rows] = s
        m_acc = macc_ref[...]
        for c in range(groups):
            m_acc = jnp.maximum(m_acc, s[:, c * LANES:(c + 1) * LANES])
        macc_ref[...] = m_acc
        m_b = m_ref[slot_b]
        s_prev = s_ref[slot_b, :, rows]
        l_acc = lacc_ref[...]
        cols = []
        for c in range(groups):
            pc = jnp.exp2(s_prev[:, c * LANES:(c + 1) * LANES] - m_b)
            l_acc = l_acc + pc
            cols.append(pc.astype(BF16))
        lacc_ref[...] = l_acc
        acc_ref[...] += _dot(jnp.concatenate(cols, axis=1), v_ref[0, rows, :])

    m_ref[slot_a] = jnp.broadcast_to(jnp.max(macc_ref[...], axis=-1, keepdims=True), (tq, LANES))
    l = jnp.sum(lacc_ref[...], axis=-1, keepdims=True)
    o_ref[0] = (acc_ref[...] / l * g_ref[0].astype(F32)).astype(o_ref.dtype)


def _attention(q, k, v, sgate, tq, kv_tile):
    b, s, _ = q.shape
    nq = s // tq
    items = b * MLA_HEADS * nq

    def item(t):
        return t // (MLA_HEADS * nq), (t // nq) % MLA_HEADS, t % nq

    def stage_a(t):
        return item(jnp.minimum(t, items - 1))

    def stage_b(t):
        return item(jnp.maximum(t - 1, 0))

    def q_map(t):
        bi, hi, qi = stage_a(t)
        return bi, qi, hi

    def k_map(t):
        bi, hi, _ = stage_a(t)
        return bi, 0, hi

    def v_map(t):
        bi, hi, _ = stage_b(t)
        return bi, 0, hi

    def o_map(t):
        bi, hi, qi = stage_b(t)
        return bi, qi, hi

    return pl.pallas_call(
        functools.partial(_attn_kernel, kv_tile=kv_tile),
        grid=(items + 1,),
        in_specs=[
            pl.BlockSpec((1, tq, QK_PAD), q_map),
            pl.BlockSpec((1, s, QK_PAD), k_map),
            pl.BlockSpec((1, s, MLA_V), v_map),
            pl.BlockSpec((1, tq, MLA_V), o_map),
        ],
        out_specs=pl.BlockSpec((1, tq, MLA_V), o_map),
        out_shape=jax.ShapeDtypeStruct((b, s, MLA_WIDTH), BF16),
        scratch_shapes=[pltpu.VMEM((2, tq, s), F32), pltpu.VMEM((2, tq, LANES), F32),
                        pltpu.VMEM((tq, LANES), F32), pltpu.VMEM((tq, LANES), F32), pltpu.VMEM((tq, MLA_V), F32)],
        compiler_params=pltpu.CompilerParams(dimension_semantics=("arbitrary",), vmem_limit_bytes=VMEM_LIMIT),
        name="mla_attention",
    )(q, k, v, sgate)


def _split3(x):
    hi = x.astype(BF16)
    r = x - hi.astype(F32)
    mid = r.astype(BF16)
    lo = (r - mid.astype(F32)).astype(BF16)
    return hi, mid, lo


def _gla_direction(q_ref, k_ref, v_ref, la_ref, o_ref, state_ref, forward):
    c = q_ref.shape[1]
    q = q_ref[0].astype(F32)
    k = k_ref[0].astype(F32)
    la = la_ref[0]
    row = lax.broadcasted_iota(jnp.int32, (c, c), 0)
    col = lax.broadcasted_iota(jnp.int32, (c, c), 1)
    tri = (col <= row) if forward else (col >= row)
    tri_b = jnp.where(tri, 1.0, 0.0).astype(BF16)
    hi, mid, lo = _split3(la)
    b = _dot(tri_b, hi) + _dot(tri_b, mid) + _dot(tri_b, lo)
    if forward:
        b_ref = b[c // 2 - 1:c // 2, :]
        b_end = b[c - 1:c, :]
    else:
        b_ref = b[c // 2:c // 2 + 1, :]
        b_end = b[0:1, :]
    q_in = (q * jnp.exp(b - b_ref)).astype(BF16)
    k_in = (k * jnp.exp(b_ref - b)).astype(BF16)
    q_st = (q * jnp.exp(b)).astype(BF16)
    k_st = (k * jnp.exp(b_end - b)).astype(BF16)
    decay = jnp.exp(b_end)
    lane_head = lax.broadcasted_iota(jnp.int32, (c, LANES), 1) // GLA_DK
    zero = jnp.zeros((c, LANES), BF16)
    for hd in range(GLA_HEADS):
        pair = slice((hd // 2) * LANES, (hd // 2 + 1) * LANES)
        mine = lane_head == (hd % 2)
        vh = v_ref[0, :, hd * GLA_DV:(hd + 1) * GLA_DV]
        a = _dot_nt(q_in[:, pair], jnp.where(mine, k_in[:, pair], zero))
        a = jnp.where(tri, a, 0.0).astype(BF16)
        st = state_ref[hd]
        o = _dot(a, vh) + _dot_nt(q_st[:, pair], st.astype(BF16))
        o_ref[0, :, hd * GLA_DV:(hd + 1) * GLA_DV] = o.astype(o_ref.dtype)
        state_ref[hd] = st * decay[:, pair] + _dot_tn(vh, jnp.where(mine, k_st[:, pair], zero))


def _gla_kernel(qf_ref, kf_ref, vf_ref, laf_ref, qb_ref, kb_ref, vb_ref, lab_ref,
                of_ref, ob_ref, sf_ref, sb_ref):
    @pl.when(pl.program_id(1) == 0)
    def _():
        sf_ref[...] = jnp.zeros_like(sf_ref)
        sb_ref[...] = jnp.zeros_like(sb_ref)

    _gla_direction(qf_ref, kf_ref, vf_ref, laf_ref, of_ref, sf_ref, True)
    _gla_direction(qb_ref, kb_ref, vb_ref, lab_ref, ob_ref, sb_ref, False)


def _gla(gq, gk, gv, la_f, la_b, chunk):
    b, s, _ = gq.shape
    n = s // chunk
    fwd = lambda w: pl.BlockSpec((1, chunk, w), lambda bi, ci: (bi, ci, 0))
    bwd = lambda w: pl.BlockSpec((1, chunk, w), lambda bi, ci: (bi, n - 1 - ci, 0))
    out = jax.ShapeDtypeStruct((b, s, GLA_WIDTH), BF16)
    state = pltpu.VMEM((GLA_HEADS, GLA_DV, LANES), F32)
    return pl.pallas_call(
        _gla_kernel,
        grid=(b, n),
        in_specs=[fwd(GLA_QK), fwd(GLA_QK), fwd(GLA_WIDTH), fwd(GLA_QK),
                  bwd(GLA_QK), bwd(GLA_QK), bwd(GLA_WIDTH), bwd(GLA_QK)],
        out_specs=[fwd(GLA_WIDTH), bwd(GLA_WIDTH)],
        out_shape=[out, out],
        scratch_shapes=[state, state],
        compiler_params=pltpu.CompilerParams(
            dimension_semantics=("arbitrary", "arbitrary"), vmem_limit_bytes=VMEM_LIMIT),
        name="gla_scan",
    )(gq, gk, gv, la_f, gq, gk, gv, la_b)


def _back_kernel(h_ref, ym_ref, of_ref, ob_ref, sgg_ref, on_ref, wout_ref, pn_ref, wpg_ref, p_ref, wpp_ref,
                 fn_ref, o_ref, *, last):
    og = of_ref[...].astype(F32) + ob_ref[...].astype(F32)
    sgg = sgg_ref[...].astype(F32)
    on = on_ref[...]
    parts = [ym_ref[...]]
    for hd in range(GLA_HEADS):
        g = slice(hd * GLA_DV, (hd + 1) * GLA_DV)
        parts.append((_rms(og[:, g], on) * sgg[:, g]).astype(BF16))
    h1 = h_ref[...] + _dot(jnp.concatenate(parts, axis=1), wout_ref[...])
    gate = jax.nn.sigmoid(_dot(_rms(h1, pn_ref[...]).astype(BF16), wpg_ref[...]))
    h2 = h1 + gate * _dot(p_ref[...].astype(BF16), wpp_ref[...])
    o_ref[...] = _rms(h2, fn_ref[...]) if last else h2


def _back(h, y_mla, o_f, o_b, sgg, out_norm, w_out, ple_norm, w_pg, p, w_pp, final_norm, tm, last):
    tokens, d = h.shape
    row = lambda n: pl.BlockSpec((tm, n), lambda i: (i, 0))
    full = lambda a: pl.BlockSpec(a.shape, lambda i: (0, 0))
    on = out_norm.reshape(1, -1)
    pn = ple_norm.reshape(1, -1)
    fn = final_norm.reshape(1, -1)
    w_out = w_out.astype(BF16)
    w_pg = w_pg.astype(BF16)
    w_pp = w_pp.astype(BF16)
    return pl.pallas_call(
        functools.partial(_back_kernel, last=last),
        grid=(tokens // tm,),
        in_specs=[row(d), row(MLA_WIDTH), row(GLA_WIDTH), row(GLA_WIDTH), row(GLA_WIDTH), full(on),
                  full(w_out), full(pn), full(w_pg), row(p.shape[1]), full(w_pp), full(fn)],
        out_specs=row(d),
        out_shape=jax.ShapeDtypeStruct((tokens, d), F32),
        compiler_params=pltpu.CompilerParams(dimension_semantics=("arbitrary",), vmem_limit_bytes=VMEM_LIMIT),
        name="back",
    )(h, y_mla, o_f, o_b, sgg, on, w_out, pn, w_pg, p, w_pp, fn)


def kernel(x, p, positions, ln_mix, w_in, mla_q_norm, w_uq, mla_kv_norm, w_ukv, gla_w_gate_fwd, gla_b_gate_fwd,
           gla_w_gate_bwd, gla_b_gate_bwd, gla_out_norm, w_out, ple_norm, w_ple_gate, w_ple_proj, final_norm):
    batch, seq, d = x.shape
    depth = w_in.shape[0]
    tokens = batch * seq
    t = _tiles(batch, seq)
    tabs = _rope_tables(positions, t["rope_rows"])
    h = x.reshape(tokens, d)
    seq3 = lambda a: a.reshape(batch, seq, a.shape[-1])
    for i in range(depth):
        weights = _front_weights(w_in[i], w_uq[i], w_ukv[i], gla_w_gate_fwd[i], gla_b_gate_fwd[i],
                                 gla_w_gate_bwd[i], gla_b_gate_bwd[i])
        q, k, v, sga, gq, gk, gv, la_f, la_b, sgg = _front(
            h, ln_mix[i], weights, mla_q_norm[i], mla_kv_norm[i], tabs, t["row_tile"])
        y_mla = _attention(seq3(q), seq3(k), seq3(v), seq3(sga), t["q_tile"], t["kv_tile"])
        o_f, o_b = _gla(seq3(gq), seq3(gk), seq3(gv), seq3(la_f), seq3(la_b), t["gla_chunk"])
        h = _back(h, y_mla.reshape(tokens, -1), o_f.reshape(tokens, -1), o_b.reshape(tokens, -1), sgg,
                  gla_out_norm[i], w_out[i], ple_norm[i], w_ple_gate[i], p[i].reshape(tokens, -1),
                  w_ple_proj[i], final_norm, t["row_tile"], last=(i == depth - 1))
    return h.reshape(batch, seq, d)
```

```python
import functools
import math

import jax
import jax.numpy as jnp
from jax import lax
from jax.experimental import pallas as pl
from jax.experimental.pallas import tpu as pltpu

EPS = 1e-6
MLA_HEADS = 4
MLA_Q_RANK = 384
MLA_KV_RANK = 256
MLA_NOPE = 128
MLA_ROPE = 64
MLA_V = 128
MLA_WIDTH = MLA_HEADS * MLA_V
ROPE_THETA = 10000.0
GLA_HEADS = 4
GLA_DK = 64
GLA_DV = 128
GLA_WIDTH = GLA_HEADS * GLA_DV
GLA_QK = GLA_HEADS * GLA_DK
GLA_GATE_RANK = 16
GLA_TAU = 16.0

LANES = 128
QK_PAD = 2 * LANES
VMEM_LIMIT = 48 * 1024 * 1024

Q_SCALE = (MLA_NOPE + MLA_ROPE) ** -0.5 * math.log2(math.e)

F32 = jnp.float32
BF16 = jnp.bfloat16


def _tiles(batch, seq):
    tokens = batch * seq
    row_tile = math.gcd(tokens, 512)
    return dict(
        row_tile=row_tile,
        q_tile=math.gcd(seq, 1024),
        kv_tile=math.gcd(seq, 2048),
        gla_chunk=math.gcd(seq, 256),
        rope_rows=math.gcd(tokens // 4, 1024),
    )


def _dot(a, b):
    return jnp.dot(a, b, preferred_element_type=F32)


def _dot_nt(a, b):
    return lax.dot_general(a, b, (((1,), (1,)), ((), ())), preferred_element_type=F32)


def _dot_tn(a, b):
    return lax.dot_general(a, b, (((0,), (0,)), ((), ())), preferred_element_type=F32)


def _rms(x, g):
    return x * lax.rsqrt(jnp.mean(x * x, axis=-1, keepdims=True) + EPS) * g


def _silu(x):
    return x * jax.nn.sigmoid(x)


def _log_sigmoid(x):
    return jnp.minimum(x, 0.0) - jnp.log1p(jnp.exp(-jnp.abs(x)))


def _rope_table_kernel(pos_ref, inv_ref, cos_ref, sin_ref):
    ang = pos_ref[...] * inv_ref[...]
    cos_ref[...] = jnp.cos(ang)
    sin_ref[...] = jnp.sin(ang)


def _rope_tables(positions, rows):
    half = MLA_ROPE // 2
    tokens = positions.size
    inv = ROPE_THETA ** (-jnp.arange(half, dtype=F32) / half)
    pos = jnp.broadcast_to(positions.reshape(tokens // 4, 4, 1).astype(F32),
                           (tokens // 4, 4, half)).reshape(tokens // 4, LANES)
    inv4 = jnp.tile(inv, 4).reshape(1, LANES)
    n = tokens // 4
    spec = pl.BlockSpec((rows, LANES), lambda i: (i, 0))
    cos, sin = pl.pallas_call(
        _rope_table_kernel,
        grid=(n // rows,),
        in_specs=[spec, pl.BlockSpec((1, LANES), lambda i: (0, 0))],
        out_specs=[spec, spec],
        out_shape=[jax.ShapeDtypeStruct((n, LANES), F32)] * 2,
        name="rope_tables",
    )(pos, inv4)
    cos = cos.reshape(tokens, half)
    sin = sin.reshape(tokens, half)
    zero = jnp.zeros_like(cos)
    tab_q = jnp.concatenate([cos, cos, -sin, sin], axis=1) * Q_SCALE
    tab_kc = jnp.concatenate([cos, cos, zero, zero], axis=1)
    tab_ks = jnp.concatenate([-sin, sin, zero, zero], axis=1)
    return tab_q, tab_kc, tab_ks


_C_CQ = 0
_C_KRLR = _C_CQ + MLA_Q_RANK
_C_CKV = _C_KRLR + LANES
_C_GA = _C_CKV + MLA_KV_RANK
_C_GQ = _C_GA + MLA_WIDTH
_C_GK = _C_GQ + GLA_QK
_C_GV = _C_GK + GLA_QK
_C_GG = _C_GV + GLA_WIDTH
_C_END = _C_GG + GLA_WIDTH
_LR_LANE = MLA_ROPE


def _front_kernel(h_ref, ln_ref, win_ref, qn_ref, wqn_ref, wqr_ref, kvn_ref, wk_ref, wv_ref,
                  wg_ref, bg_ref, tq_ref, tkc_ref, tks_ref,
                  q_ref, k_ref, v_ref, sga_ref, gq_ref, gk_ref, gv_ref, laf_ref, lab_ref, sgg_ref):
    xn = _rms(h_ref[...], ln_ref[...]).astype(BF16)

    def proj(c0, c1):
        return _dot(xn, win_ref[:, c0:c1])

    cq_krlr = proj(_C_CQ, _C_CKV)
    krlr = cq_krlr[:, _C_KRLR:_C_CKV]

    cqn = _rms(cq_krlr[:, :_C_KRLR], qn_ref[...]).astype(BF16)
    q_nope = _dot(cqn, wqn_ref[...]) * Q_SCALE
    q_rope = _dot(cqn, wqr_ref[...])
    tq = tq_ref[...]
    for hd in range(MLA_HEADS):
        g = slice(hd * LANES, (hd + 1) * LANES)
        q_ref[:, hd * QK_PAD:hd * QK_PAD + LANES] = q_nope[:, g].astype(BF16)
        qr = q_rope[:, g] * tq
        q_ref[:, hd * QK_PAD + LANES:(hd + 1) * QK_PAD] = (qr + pltpu.roll(qr, MLA_ROPE, axis=1)).astype(BF16)

    ckvn = _rms(proj(_C_CKV, _C_GA), kvn_ref[...]).astype(BF16)
    k_nope = _dot(ckvn, wk_ref[...])
    v = _dot(ckvn, wv_ref[...]).astype(BF16)
    ones = jnp.ones((v.shape[0], MLA_V), BF16)
    for hd in range(MLA_HEADS):
        v_ref[:, 2 * hd * MLA_V:(2 * hd + 1) * MLA_V] = v[:, hd * MLA_V:(hd + 1) * MLA_V]
        v_ref[:, (2 * hd + 1) * MLA_V:(2 * hd + 2) * MLA_V] = ones
    half = MLA_ROPE // 2
    lane = lax.broadcasted_iota(jnp.int32, krlr.shape, 1)
    kr_sw = jnp.where(lane < half, pltpu.roll(krlr, LANES - half, axis=1), pltpu.roll(krlr, half, axis=1))
    kr = (krlr * tkc_ref[...] + kr_sw * tks_ref[...]).astype(BF16)
    for hd in range(MLA_HEADS):
        k_ref[:, hd * QK_PAD:hd * QK_PAD + LANES] = k_nope[:, hd * LANES:(hd + 1) * LANES].astype(BF16)
        k_ref[:, hd * QK_PAD + LANES:(hd + 1) * QK_PAD] = kr

    sga_ref[...] = _silu(proj(_C_GA, _C_GQ)).astype(BF16)
    sgg_ref[...] = _silu(proj(_C_GG, _C_END)).astype(BF16)

    gq_ref[...] = (proj(_C_GQ, _C_GK) * GLA_DK ** -0.5).astype(BF16)
    gk_ref[...] = proj(_C_GK, _C_GV).astype(BF16)
    gv_ref[...] = proj(_C_GV, _C_GG).astype(BF16)
    la = _log_sigmoid(_dot(krlr.astype(BF16), wg_ref[...]) + bg_ref[...]) / GLA_TAU
    laf_ref[...] = la[:, :GLA_QK]
    lab_ref[...] = la[:, GLA_QK:]


def _front_weights(w_in, w_uq, w_ukv, wgf, bgf, wgb, bgb):
    d = w_in.shape[0]
    c = 0
    parts = {}
    for name, size in (("cq", MLA_Q_RANK), ("ckv", MLA_KV_RANK), ("kr", MLA_ROPE), ("ga", MLA_WIDTH),
                       ("gq", GLA_QK), ("gk", GLA_QK), ("gv", GLA_WIDTH), ("lrf", GLA_GATE_RANK),
                       ("lrb", GLA_GATE_RANK), ("gg", GLA_WIDTH)):
        parts[name] = w_in[:, c:c + size]
        c += size
    half = MLA_ROPE // 2
    swap = lambda w: jnp.concatenate([w[..., half:], w[..., :half]], axis=-1)
    zeros = lambda n: jnp.zeros((d, n), w_in.dtype)
    win = jnp.concatenate([
        parts["cq"],
        parts["kr"], parts["lrf"], parts["lrb"], zeros(LANES - MLA_ROPE - 2 * GLA_GATE_RANK),
        parts["ckv"], parts["ga"], parts["gq"], parts["gk"], parts["gv"], parts["gg"]], axis=1).astype(BF16)

    wq = w_uq.reshape(MLA_Q_RANK, MLA_HEADS, MLA_NOPE + MLA_ROPE)
    wqn = wq[:, :, :MLA_NOPE].reshape(MLA_Q_RANK, MLA_HEADS * MLA_NOPE).astype(BF16)
    wr = wq[:, :, MLA_NOPE:]
    wqr = jnp.concatenate([wr, swap(wr)], axis=-1).reshape(MLA_Q_RANK, MLA_HEADS * LANES).astype(BF16)

    wkv = w_ukv.reshape(MLA_KV_RANK, MLA_HEADS, MLA_NOPE + MLA_V)
    wk = wkv[:, :, :MLA_NOPE].reshape(MLA_KV_RANK, MLA_HEADS * MLA_NOPE).astype(BF16)
    wv = wkv[:, :, MLA_NOPE:].reshape(MLA_KV_RANK, MLA_HEADS * MLA_V).astype(BF16)

    wg = jnp.zeros((LANES, 2 * GLA_QK), F32)
    wg = wg.at[_LR_LANE:_LR_LANE + GLA_GATE_RANK, :GLA_QK].set(wgf)
    wg = wg.at[_LR_LANE + GLA_GATE_RANK:_LR_LANE + 2 * GLA_GATE_RANK, GLA_QK:].set(wgb)
    bg = jnp.concatenate([bgf, bgb]).reshape(1, 2 * GLA_QK)
    return win, wqn, wqr, wk, wv, wg.astype(BF16), bg


def _front(h, ln, weights, qn, kvn, tabs, tm):
    tokens, d = h.shape
    win, wqn, wqr, wk, wv, wg, bg = weights
    tq, tkc, tks = tabs
    row = lambda n: pl.BlockSpec((tm, n), lambda i: (i, 0))
    full = lambda a: pl.BlockSpec(a.shape, lambda i: (0, 0))
    ln = ln.reshape(1, -1)
    qn = qn.reshape(1, -1)
    kvn = kvn.reshape(1, -1)
    outs = [
        (MLA_HEADS * QK_PAD, BF16),
        (MLA_HEADS * QK_PAD, BF16),
        (2 * MLA_WIDTH, BF16),
        (MLA_WIDTH, BF16),
        (GLA_QK, BF16),
        (GLA_QK, BF16),
        (GLA_WIDTH, BF16),
        (GLA_QK, F32),
        (GLA_QK, F32),
        (GLA_WIDTH, BF16),
    ]
    return pl.pallas_call(
        _front_kernel,
        grid=(tokens // tm,),
        in_specs=[row(d), full(ln), full(win), full(qn), full(wqn), full(wqr), full(kvn),
                  full(wk), full(wv), full(wg), full(bg), row(LANES), row(LANES), row(LANES)],
        out_specs=[row(n) for n, _ in outs],
        out_shape=[jax.ShapeDtypeStruct((tokens, n), dt) for n, dt in outs],
        compiler_params=pltpu.CompilerParams(dimension_semantics=("arbitrary",), vmem_limit_bytes=VMEM_LIMIT),
        name="front",
    )(h, ln, win, qn, wqn, wqr, kvn, wk, wv, wg, bg, tq, tkc, tks)


def _attn_kernel(q_ref, k_ref, v_ref, g_ref, o_ref, s_ref, macc_ref, acc_ref, *, kv_tile):
    tq = q_ref.shape[1]
    n_kv = k_ref.shape[1] // kv_tile
    groups = kv_tile // LANES
    macc_ref[...] = jnp.full((tq, LANES), -jnp.inf, F32)
    acc_ref[...] = jnp.zeros((tq, 2 * MLA_V), F32)

    @pl.loop(0, n_kv)
    def _(j):
        rows = pl.ds(pl.multiple_of(j * kv_tile, kv_tile), kv_tile)
        s = _dot_nt(q_ref[0], k_ref[0, rows, :])
        s_ref[:, rows] = s
        m_acc = macc_ref[...]
        for c in range(groups):
            m_acc = jnp.maximum(m_acc, s[:, c * LANES:(c + 1) * LANES])
        macc_ref[...] = m_acc

    m_b = jnp.broadcast_to(jnp.max(macc_ref[...], axis=-1, keepdims=True), (tq, LANES))

    @pl.loop(0, n_kv)
    def _(j):
        rows = pl.ds(pl.multiple_of(j * kv_tile, kv_tile), kv_tile)
        s_blk = s_ref[:, rows]
        cols = [jnp.exp2((s_blk[:, c * LANES:(c + 1) * LANES] - m_b).astype(BF16)) for c in range(groups)]
        acc_ref[...] += _dot(jnp.concatenate(cols, axis=1), v_ref[0, rows, :])

    acc = acc_ref[...]
    o_ref[0] = (acc[:, :MLA_V] / acc[:, MLA_V:] * g_ref[0].astype(F32)).astype(o_ref.dtype)


def _attention(q, k, v, sgate, tq, kv_tile):
    b, s, _ = q.shape
    return pl.pallas_call(
        functools.partial(_attn_kernel, kv_tile=kv_tile),
        grid=(b, MLA_HEADS, s // tq),
        in_specs=[
            pl.BlockSpec((1, tq, QK_PAD), lambda bi, hi, qi: (bi, qi, hi)),
            pl.BlockSpec((1, s, QK_PAD), lambda bi, hi, qi: (bi, 0, hi)),
            pl.BlockSpec((1, s, 2 * MLA_V), lambda bi, hi, qi: (bi, 0, hi)),
            pl.BlockSpec((1, tq, MLA_V), lambda bi, hi, qi: (bi, qi, hi)),
        ],
        out_specs=pl.BlockSpec((1, tq, MLA_V), lambda bi, hi, qi: (bi, qi, hi)),
        out_shape=jax.ShapeDtypeStruct((b, s, MLA_WIDTH), BF16),
        scratch_shapes=[pltpu.VMEM((tq, s), F32), pltpu.VMEM((tq, LANES), F32), pltpu.VMEM((tq, 2 * MLA_V), F32)],
        compiler_params=pltpu.CompilerParams(
            dimension_semantics=("arbitrary", "arbitrary", "arbitrary"), vmem_limit_bytes=VMEM_LIMIT),
        name="mla_attention",
    )(q, k, v, sgate)


def _split2(x):
    hi = x.astype(BF16)
    lo = (x - hi.astype(F32)).astype(BF16)
    return hi, lo


def _gla_direction(q_ref, k_ref, v_ref, la_ref, o_ref, state_ref, forward):
    c = q_ref.shape[1]
    q = q_ref[0].astype(F32)
    k = k_ref[0].astype(F32)
    la = la_ref[0]
    row = lax.broadcasted_iota(jnp.int32, (c, c), 0)
    col = lax.broadcasted_iota(jnp.int32, (c, c), 1)
    tri = (col <= row) if forward else (col >= row)
    tri_b = jnp.where(tri, 1.0, 0.0).astype(BF16)
    hi, lo = _split2(la)
    b = _dot(tri_b, jnp.concatenate([hi, lo], axis=1))
    b = b[:, :GLA_QK] + b[:, GLA_QK:]
    if forward:
        b_ref = b[c // 2 - 1:c // 2, :]
        b_end = b[c - 1:c, :]
    else:
        b_ref = b[c // 2:c // 2 + 1, :]
        b_end = b[0:1, :]
    q_in = (q * jnp.exp(b - b_ref)).astype(BF16)
    k_in = (k * jnp.exp(b_ref - b)).astype(BF16)
    q_st = (q * jnp.exp(b)).astype(BF16)
    k_st = (k * jnp.exp(b_end - b)).astype(BF16)
    decay = jnp.exp(b_end)
    lane_head = lax.broadcasted_iota(jnp.int32, (c, LANES), 1) // GLA_DK
    zero = jnp.zeros((c, LANES), BF16)
    for hd in range(GLA_HEADS):
        pair = slice((hd // 2) * LANES, (hd // 2 + 1) * LANES)
        mine = lane_head == (hd % 2)
        vh = v_ref[0, :, hd * GLA_DV:(hd + 1) * GLA_DV]
        a = _dot_nt(q_in[:, pair], jnp.where(mine, k_in[:, pair], zero))
        a = jnp.where(tri, a, 0.0).astype(BF16)
        st = state_ref[hd]
        o = _dot(a, vh) + _dot_nt(q_st[:, pair], st.astype(BF16))
        o_ref[0, :, hd * GLA_DV:(hd + 1) * GLA_DV] = o.astype(o_ref.dtype)
        state_ref[hd] = st * decay[:, pair] + _dot_tn(vh, jnp.where(mine, k_st[:, pair], zero))


def _gla_kernel(qf_ref, kf_ref, vf_ref, laf_ref, qb_ref, kb_ref, vb_ref, lab_ref,
                of_ref, ob_ref, sf_ref, sb_ref):
    @pl.when(pl.program_id(1) == 0)
    def _():
        sf_ref[...] = jnp.zeros_like(sf_ref)
        sb_ref[...] = jnp.zeros_like(sb_ref)

    _gla_direction(qf_ref, kf_ref, vf_ref, laf_ref, of_ref, sf_ref, True)
    _gla_direction(qb_ref, kb_ref, vb_ref, lab_ref, ob_ref, sb_ref, False)


def _gla(gq, gk, gv, la_f, la_b, chunk):
    b, s, _ = gq.shape
    n = s // chunk
    fwd = lambda w: pl.BlockSpec((1, chunk, w), lambda bi, ci: (bi, ci, 0))
    bwd = lambda w: pl.BlockSpec((1, chunk, w), lambda bi, ci: (bi, n - 1 - ci, 0))
    out = jax.ShapeDtypeStruct((b, s, GLA_WIDTH), BF16)
    state = pltpu.VMEM((GLA_HEADS, GLA_DV, LANES), F32)
    return pl.pallas_call(
        _gla_kernel,
        grid=(b, n),
        in_specs=[fwd(GLA_QK), fwd(GLA_QK), fwd(GLA_WIDTH), fwd(GLA_QK),
                  bwd(GLA_QK), bwd(GLA_QK), bwd(GLA_WIDTH), bwd(GLA_QK)],
        out_specs=[fwd(GLA_WIDTH), bwd(GLA_WIDTH)],
        out_shape=[out, out],
        scratch_shapes=[state, state],
        compiler_params=pltpu.CompilerParams(
            dimension_semantics=("arbitrary", "arbitrary"), vmem_limit_bytes=VMEM_LIMIT),
        name="gla_scan",
    )(gq, gk, gv, la_f, gq, gk, gv, la_b)


def _back_kernel(h_ref, ym_ref, of_ref, ob_ref, sgg_ref, on_ref, wout_ref, pn_ref, wpg_ref, p_ref, wpp_ref,
                 fn_ref, o_ref, *, last):
    og = of_ref[...].astype(F32) + ob_ref[...].astype(F32)
    sgg = sgg_ref[...].astype(F32)
    on = on_ref[...]
    parts = [ym_ref[...]]
    for hd in range(GLA_HEADS):
        g = slice(hd * GLA_DV, (hd + 1) * GLA_DV)
        parts.append((_rms(og[:, g], on) * sgg[:, g]).astype(BF16))
    h1 = h_ref[...] + _dot(jnp.concatenate(parts, axis=1), wout_ref[...])
    gate = jax.nn.sigmoid(_dot(_rms(h1, pn_ref[...]).astype(BF16), wpg_ref[...]))
    h2 = h1 + gate * _dot(p_ref[...].astype(BF16), wpp_ref[...])
    o_ref[...] = _rms(h2, fn_ref[...]) if last else h2


def _back(h, y_mla, o_f, o_b, sgg, out_norm, w_out, ple_norm, w_pg, p, w_pp, final_norm, tm, last):
    tokens, d = h.shape
    row = lambda n: pl.BlockSpec((tm, n), lambda i: (i, 0))
    full = lambda a: pl.BlockSpec(a.shape, lambda i: (0, 0))
    on = out_norm.reshape(1, -1)
    pn = ple_norm.reshape(1, -1)
    fn = final_norm.reshape(1, -1)
    w_out = w_out.astype(BF16)
    w_pg = w_pg.astype(BF16)
    w_pp = w_pp.astype(BF16)
    return pl.pallas_call(
        functools.partial(_back_kernel, last=last),
        grid=(tokens // tm,),
        in_specs=[row(d), row(MLA_WIDTH), row(GLA_WIDTH), row(GLA_WIDTH), row(GLA_WIDTH), full(on),
                  full(w_out), full(pn), full(w_pg), row(p.shape[1]), full(w_pp), full(fn)],
        out_specs=row(d),
        out_shape=jax.ShapeDtypeStruct((tokens, d), F32),
        compiler_params=pltpu.CompilerParams(dimension_semantics=("arbitrary",), vmem_limit_bytes=VMEM_LIMIT),
        name="back",
    )(h, y_mla, o_f, o_b, sgg, on, w_out, pn, w_pg, p, w_pp, fn)


def kernel(x, p, positions, ln_mix, w_in, mla_q_norm, w_uq, mla_kv_norm, w_ukv, gla_w_gate_fwd, gla_b_gate_fwd,
           gla_w_gate_bwd, gla_b_gate_bwd, gla_out_norm, w_out, ple_norm, w_ple_gate, w_ple_proj, final_norm):
    batch, seq, d = x.shape
    depth = w_in.shape[0]
    tokens = batch * seq
    t = _tiles(batch, seq)
    tabs = _rope_tables(positions, t["rope_rows"])
    h = x.reshape(tokens, d)
    seq3 = lambda a: a.reshape(batch, seq, a.shape[-1])
    for i in range(depth):
        weights = _front_weights(w_in[i], w_uq[i], w_ukv[i], gla_w_gate_fwd[i], gla_b_gate_fwd[i],
                                 gla_w_gate_bwd[i], gla_b_gate_bwd[i])
        q, k, v, sga, gq, gk, gv, la_f, la_b, sgg = _front(
            h, ln_mix[i], weights, mla_q_norm[i], mla_kv_norm[i], tabs, t["row_tile"])
        y_mla = _attention(seq3(q), seq3(k), seq3(v), seq3(sga), t["q_tile"], t["kv_tile"])
        o_f, o_b = _gla(seq3(gq), seq3(gk), seq3(gv), seq3(la_f), seq3(la_b), t["gla_chunk"])
        h = _back(h, y_mla.reshape(tokens, -1), o_f.reshape(tokens, -1), o_b.reshape(tokens, -1), sgg,
                  gla_out_norm[i], w_out[i], ple_norm[i], w_ple_gate[i], p[i].reshape(tokens, -1),
                  w_ple_proj[i], final_norm, t["row_tile"], last=(i == depth - 1))
    return h.reshape(batch, seq, d)
```

```python
import functools
import math

import jax
import jax.numpy as jnp
from jax import lax
from jax.experimental import pallas as pl
from jax.experimental.pallas import tpu as pltpu

EPS = 1e-6
MLA_HEADS = 4
MLA_Q_RANK = 384
MLA_KV_RANK = 256
MLA_NOPE = 128
MLA_ROPE = 64
MLA_V = 128
MLA_WIDTH = MLA_HEADS * MLA_V
ROPE_THETA = 10000.0
GLA_HEADS = 4
GLA_DK = 64
GLA_DV = 128
GLA_WIDTH = GLA_HEADS * GLA_DV
GLA_QK = GLA_HEADS * GLA_DK
GLA_GATE_RANK = 16
GLA_TAU = 16.0

LANES = 128
QK_PAD = 2 * LANES
VMEM_LIMIT = 48 * 1024 * 1024

Q_SCALE = (MLA_NOPE + MLA_ROPE) ** -0.5 * math.log2(math.e)

F32 = jnp.float32
BF16 = jnp.bfloat16


def _tiles(batch, seq):
    tokens = batch * seq
    row_tile = math.gcd(tokens, 512)
    return dict(
        row_tile=row_tile,
        q_tile=math.gcd(seq, 1024),
        kv_tile=math.gcd(seq, 4096),
        gla_chunk=math.gcd(seq, 256),
        rope_rows=math.gcd(tokens // 4, 1024),
    )


def _dot(a, b):
    return jnp.dot(a, b, preferred_element_type=F32)


def _dot_nt(a, b):
    return lax.dot_general(a, b, (((1,), (1,)), ((), ())), preferred_element_type=F32)


def _dot_tn(a, b):
    return lax.dot_general(a, b, (((0,), (0,)), ((), ())), preferred_element_type=F32)


def _rms(x, g):
    return x * lax.rsqrt(jnp.mean(x * x, axis=-1, keepdims=True) + EPS) * g


def _silu(x):
    return x * jax.nn.sigmoid(x)


def _log_sigmoid(x):
    return jnp.minimum(x, 0.0) - jnp.log1p(jnp.exp(-jnp.abs(x)))


def _split2(x):
    hi = x.astype(BF16)
    lo = (x - hi.astype(F32)).astype(BF16)
    return hi, lo


def _rope_table_kernel(pos_ref, inv_ref, tq_ref, tkc_ref, tks_ref):
    half = MLA_ROPE // 2
    ang = pos_ref[...] * inv_ref[...]
    cos = jnp.cos(ang)
    sin = jnp.sin(ang)
    lane = lax.broadcasted_iota(jnp.int32, cos.shape, 1)
    for i in range(LANES // half):
        place = lambda x, dst: x if dst == i else pltpu.roll(x, ((dst - i) * half) % LANES, axis=1)
        base = jnp.where(lane < half, place(cos, 0),
                         jnp.where(lane < 2 * half, place(cos, 1),
                                   jnp.where(lane < 3 * half, -place(sin, 2), place(sin, 3))))
        tq_ref[i] = base * Q_SCALE
        tkc_ref[i] = jnp.where(lane < 2 * half, base, 0.0)
        tks_ref[i] = jnp.where(lane < 2 * half, pltpu.roll(base, 2 * half, axis=1), 0.0)


def _rope_tables(positions, rows):
    half = MLA_ROPE // 2
    per_row = LANES // half
    tokens = positions.size
    n = tokens // per_row
    inv = ROPE_THETA ** (-jnp.arange(half, dtype=F32) / half)
    pos = jnp.broadcast_to(positions.reshape(per_row, n, 1).astype(F32), (per_row, n, half))
    pos = pos.transpose(1, 0, 2).reshape(n, LANES)
    inv4 = jnp.tile(inv, per_row).reshape(1, LANES)
    out_spec = pl.BlockSpec((per_row, rows, LANES), lambda i: (0, i, 0))
    tabs = pl.pallas_call(
        _rope_table_kernel,
        grid=(n // rows,),
        in_specs=[pl.BlockSpec((rows, LANES), lambda i: (i, 0)), pl.BlockSpec((1, LANES), lambda i: (0, 0))],
        out_specs=[out_spec] * 3,
        out_shape=[jax.ShapeDtypeStruct((per_row, n, LANES), F32)] * 3,
        name="rope_tables",
    )(pos, inv4)
    return tuple(t.reshape(tokens, LANES) for t in tabs)


_C_CQ = 0
_C_KRLR = _C_CQ + MLA_Q_RANK
_C_CKV = _C_KRLR + LANES
_C_GA = _C_CKV + MLA_KV_RANK
_C_GQ = _C_GA + MLA_WIDTH
_C_GK = _C_GQ + GLA_QK
_C_GV = _C_GK + GLA_QK
_C_GG = _C_GV + GLA_WIDTH
_C_END = _C_GG + GLA_WIDTH
_LR_LANE = MLA_ROPE


def _front_kernel(h_ref, ln_ref, win_ref, qn_ref, wqn_ref, wqr_ref, kvn_ref, wk_ref, wv_ref,
                  wg_ref, bg_ref, tq_ref, tkc_ref, tks_ref,
                  q_ref, k_ref, v_ref, sga_ref, gf_ref, gb_ref, dec_ref, gv_ref, sgg_ref, *, chunk):
    xn = _rms(h_ref[...], ln_ref[...]).astype(BF16)

    def proj(c0, c1):
        return _dot(xn, win_ref[:, c0:c1])

    cq_krlr = proj(_C_CQ, _C_CKV)
    krlr = cq_krlr[:, _C_KRLR:_C_CKV]

    cqn = _rms(cq_krlr[:, :_C_KRLR], qn_ref[...]).astype(BF16)
    q_nope = _dot(cqn, wqn_ref[...]) * Q_SCALE
    q_rope = _dot(cqn, wqr_ref[...])
    tq = tq_ref[...]
    for hd in range(MLA_HEADS):
        g = slice(hd * LANES, (hd + 1) * LANES)
        q_ref[:, hd * QK_PAD:hd * QK_PAD + LANES] = q_nope[:, g].astype(BF16)
        qr = q_rope[:, g] * tq
        q_ref[:, hd * QK_PAD + LANES:(hd + 1) * QK_PAD] = (qr + pltpu.roll(qr, MLA_ROPE, axis=1)).astype(BF16)

    ckvn = _rms(proj(_C_CKV, _C_GA), kvn_ref[...]).astype(BF16)
    k_nope = _dot(ckvn, wk_ref[...])
    v = _dot(ckvn, wv_ref[...]).astype(BF16)
    ones = jnp.ones((v.shape[0], MLA_V), BF16)
    for hd in range(MLA_HEADS):
        v_ref[:, 2 * hd * MLA_V:(2 * hd + 1) * MLA_V] = v[:, hd * MLA_V:(hd + 1) * MLA_V]
        v_ref[:, (2 * hd + 1) * MLA_V:(2 * hd + 2) * MLA_V] = ones
    half = MLA_ROPE // 2
    lane = lax.broadcasted_iota(jnp.int32, krlr.shape, 1)
    kr_sw = jnp.where(lane < half, pltpu.roll(krlr, LANES - half, axis=1), pltpu.roll(krlr, half, axis=1))
    kr = (krlr * tkc_ref[...] + kr_sw * tks_ref[...]).astype(BF16)
    for hd in range(MLA_HEADS):
        k_ref[:, hd * QK_PAD:hd * QK_PAD + LANES] = k_nope[:, hd * LANES:(hd + 1) * LANES].astype(BF16)
        k_ref[:, hd * QK_PAD + LANES:(hd + 1) * QK_PAD] = kr

    sga_ref[...] = _silu(proj(_C_GA, _C_GQ)).astype(BF16)
    sgg_ref[...] = _silu(proj(_C_GG, _C_END)).astype(BF16)

    gq = proj(_C_GQ, _C_GK) * GLA_DK ** -0.5
    gk = proj(_C_GK, _C_GV)
    gv_ref[...] = proj(_C_GV, _C_GG).astype(BF16)
    la = _log_sigmoid(_dot(krlr.astype(BF16), wg_ref[...]) + bg_ref[...]) * (math.log2(math.e) / GLA_TAU)
    row = lax.broadcasted_iota(jnp.int32, (chunk, chunk), 0)
    col = lax.broadcasted_iota(jnp.int32, (chunk, chunk), 1)
    for d, out_ref in enumerate((gf_ref, gb_ref)):
        forward = d == 0
        tri = jnp.where((col <= row) if forward else (col >= row), 1.0, 0.0).astype(BF16)
        for ci in range(la.shape[0] // chunk):
            r = slice(ci * chunk, (ci + 1) * chunk)
            hi, lo = _split2(la[r, d * GLA_QK:(d + 1) * GLA_QK])
            b = _dot(tri, jnp.concatenate([hi, lo], axis=1))
            b = b[:, :GLA_QK] + b[:, GLA_QK:]
            mid = chunk // 2 - 1 if forward else chunk // 2
            end = chunk - 1 if forward else 0
            b_mid = b[mid:mid + 1, :]
            b_end = b[end:end + 1, :]
            q_in = gq[r] * jnp.exp2(b - b_mid)
            k_in = gk[r] * jnp.exp2(b_mid - b)
            out_ref[r, 0 * GLA_QK:1 * GLA_QK] = q_in.astype(BF16)
            out_ref[r, 1 * GLA_QK:2 * GLA_QK] = k_in.astype(BF16)
            out_ref[r, 2 * GLA_QK:3 * GLA_QK] = (q_in * jnp.exp2(b_mid)).astype(BF16)
            out_ref[r, 3 * GLA_QK:4 * GLA_QK] = (k_in * jnp.exp2(b_end - b_mid)).astype(BF16)
            dec_ref[ci, :, d * GLA_QK:(d + 1) * GLA_QK] = jnp.exp2(b_end)


def _front_weights(w_in, w_uq, w_ukv, wgf, bgf, wgb, bgb):
    d = w_in.shape[0]
    c = 0
    parts = {}
    for name, size in (("cq", MLA_Q_RANK), ("ckv", MLA_KV_RANK), ("kr", MLA_ROPE), ("ga", MLA_WIDTH),
                       ("gq", GLA_QK), ("gk", GLA_QK), ("gv", GLA_WIDTH), ("lrf", GLA_GATE_RANK),
                       ("lrb", GLA_GATE_RANK), ("gg", GLA_WIDTH)):
        parts[name] = w_in[:, c:c + size]
        c += size
    half = MLA_ROPE // 2
    swap = lambda w: jnp.concatenate([w[..., half:], w[..., :half]], axis=-1)
    zeros = lambda n: jnp.zeros((d, n), w_in.dtype)
    win = jnp.concatenate([
        parts["cq"],
        parts["kr"], parts["lrf"], parts["lrb"], zeros(LANES - MLA_ROPE - 2 * GLA_GATE_RANK),
        parts["ckv"], parts["ga"], parts["gq"], parts["gk"], parts["gv"], parts["gg"]], axis=1).astype(BF16)

    wq = w_uq.reshape(MLA_Q_RANK, MLA_HEADS, MLA_NOPE + MLA_ROPE)
    wqn = wq[:, :, :MLA_NOPE].reshape(MLA_Q_RANK, MLA_HEADS * MLA_NOPE).astype(BF16)
    wr = wq[:, :, MLA_NOPE:]
    wqr = jnp.concatenate([wr, swap(wr)], axis=-1).reshape(MLA_Q_RANK, MLA_HEADS * LANES).astype(BF16)

    wkv = w_ukv.reshape(MLA_KV_RANK, MLA_HEADS, MLA_NOPE + MLA_V)
    wk = wkv[:, :, :MLA_NOPE].reshape(MLA_KV_RANK, MLA_HEADS * MLA_NOPE).astype(BF16)
    wv = wkv[:, :, MLA_NOPE:].reshape(MLA_KV_RANK, MLA_HEADS * MLA_V).astype(BF16)

    wg = jnp.zeros((LANES, 2 * GLA_QK), F32)
    wg = wg.at[_LR_LANE:_LR_LANE + GLA_GATE_RANK, :GLA_QK].set(wgf)
    wg = wg.at[_LR_LANE + GLA_GATE_RANK:_LR_LANE + 2 * GLA_GATE_RANK, GLA_QK:].set(wgb)
    bg = jnp.concatenate([bgf, bgb]).reshape(1, 2 * GLA_QK)
    return win, wqn, wqr, wk, wv, wg.astype(BF16), bg


def _front(h, ln, weights, qn, kvn, tabs, tm, chunk):
    tokens, d = h.shape
    assert tm % chunk == 0, (tm, chunk)
    win, wqn, wqr, wk, wv, wg, bg = weights
    tq, tkc, tks = tabs
    row = lambda n: pl.BlockSpec((tm, n), lambda i: (i, 0))
    full = lambda a: pl.BlockSpec(a.shape, lambda i: (0, 0))
    ln = ln.reshape(1, -1)
    qn = qn.reshape(1, -1)
    kvn = kvn.reshape(1, -1)
    outs = [
        (MLA_HEADS * QK_PAD, BF16),
        (MLA_HEADS * QK_PAD, BF16),
        (2 * MLA_WIDTH, BF16),
        (MLA_WIDTH, BF16),
        (4 * GLA_QK, BF16),
        (4 * GLA_QK, BF16),
        None,
        (GLA_WIDTH, BF16),
        (GLA_WIDTH, BF16),
    ]
    dec_spec = pl.BlockSpec((tm // chunk, 1, 2 * GLA_QK), lambda i: (i, 0, 0))
    dec_shape = jax.ShapeDtypeStruct((tokens // chunk, 1, 2 * GLA_QK), F32)
    return pl.pallas_call(
        functools.partial(_front_kernel, chunk=chunk),
        grid=(tokens // tm,),
        in_specs=[row(d), full(ln), full(win), full(qn), full(wqn), full(wqr), full(kvn),
                  full(wk), full(wv), full(wg), full(bg), row(LANES), row(LANES), row(LANES)],
        out_specs=[dec_spec if o is None else row(o[0]) for o in outs],
        out_shape=[dec_shape if o is None else jax.ShapeDtypeStruct((tokens, o[0]), o[1]) for o in outs],
        compiler_params=pltpu.CompilerParams(dimension_semantics=("arbitrary",), vmem_limit_bytes=VMEM_LIMIT),
        name="front",
    )(h, ln, win, qn, wqn, wqr, kvn, wk, wv, wg, bg, tq, tkc, tks)


def _attn_kernel(q_ref, k_ref, v_ref, g_ref, o_ref, s_ref, macc_ref, acc_ref, *, kv_tile):
    tq = q_ref.shape[1]
    n_kv = k_ref.shape[1] // kv_tile
    groups = kv_tile // LANES
    n_dyn = jnp.minimum(pl.program_id(0) + n_kv, n_kv)
    macc_ref[...] = jnp.full((tq, LANES), -jnp.inf, F32)
    acc_ref[...] = jnp.zeros((tq, 2 * MLA_V), F32)

    @pl.loop(0, n_dyn)
    def _(j):
        rows = pl.ds(pl.multiple_of(j * kv_tile, kv_tile), kv_tile)
        s = _dot_nt(q_ref[0], k_ref[0, rows, :])
        s_ref[:, rows] = s
        m_acc = macc_ref[...]
        for c in range(groups):
            m_acc = jnp.maximum(m_acc, s[:, c * LANES:(c + 1) * LANES])
        macc_ref[...] = m_acc

    m_b = jnp.broadcast_to(jnp.max(macc_ref[...], axis=-1, keepdims=True), (tq, LANES))

    @pl.loop(0, n_dyn)
    def _(j):
        rows = pl.ds(pl.multiple_of(j * kv_tile, kv_tile), kv_tile)
        s_blk = s_ref[:, rows]
        cols = [jnp.exp2((s_blk[:, c * LANES:(c + 1) * LANES] - m_b).astype(BF16)) for c in range(groups)]
        acc_ref[...] += _dot(jnp.concatenate(cols, axis=1), v_ref[0, rows, :])

    acc = acc_ref[...]
    o_ref[0] = (acc[:, :MLA_V] / acc[:, MLA_V:] * g_ref[0].astype(F32)).astype(o_ref.dtype)


def _attention(q, k, v, sgate, tq, kv_tile):
    b, s, _ = q.shape
    return pl.pallas_call(
        functools.partial(_attn_kernel, kv_tile=kv_tile),
        grid=(b, MLA_HEADS, s // tq),
        in_specs=[
            pl.BlockSpec((1, tq, QK_PAD), lambda bi, hi, qi: (bi, qi, hi)),
            pl.BlockSpec((1, s, QK_PAD), lambda bi, hi, qi: (bi, 0, hi)),
            pl.BlockSpec((1, s, 2 * MLA_V), lambda bi, hi, qi: (bi, 0, hi)),
            pl.BlockSpec((1, tq, MLA_V), lambda bi, hi, qi: (bi, qi, hi)),
        ],
        out_specs=pl.BlockSpec((1, tq, MLA_V), lambda bi, hi, qi: (bi, qi, hi)),
        out_shape=jax.ShapeDtypeStruct((b, s, MLA_WIDTH), BF16),
        scratch_shapes=[pltpu.VMEM((tq, s), F32), pltpu.VMEM((tq, LANES), F32), pltpu.VMEM((tq, 2 * MLA_V), F32)],
        compiler_params=pltpu.CompilerParams(
            dimension_semantics=("arbitrary", "arbitrary", "arbitrary"), vmem_limit_bytes=VMEM_LIMIT),
        name="mla_attention",
    )(q, k, v, sgate)


def _gla_direction(g_ref, v_ref, decay, o_ref, state_ref, forward):
    c = g_ref.shape[1]
    row = lax.broadcasted_iota(jnp.int32, (c, c), 0)
    col = lax.broadcasted_iota(jnp.int32, (c, c), 1)
    tri = (col <= row) if forward else (col >= row)
    lane_head = lax.broadcasted_iota(jnp.int32, (c, LANES), 1) // GLA_DK
    zero = jnp.zeros((c, LANES), BF16)
    for hd in range(GLA_HEADS):
        pair = (hd // 2) * LANES
        part = lambda i: g_ref[0, :, i * GLA_QK + pair:i * GLA_QK + pair + LANES]
        mine = lane_head == (hd % 2)
        vh = v_ref[0, :, hd * GLA_DV:(hd + 1) * GLA_DV]
        a = _dot_nt(part(0), jnp.where(mine, part(1), zero))
        a = jnp.where(tri, a, 0.0).astype(BF16)
        st = state_ref[hd]
        o = _dot(a, vh) + _dot_nt(part(2), st.astype(BF16))
        o_ref[0, :, hd * GLA_DV:(hd + 1) * GLA_DV] = o.astype(o_ref.dtype)
        state_ref[hd] = st * decay[:, pair:pair + LANES] + _dot_tn(vh, jnp.where(mine, part(3), zero))


def _gla_kernel(gf_ref, vf_ref, df_ref, gb_ref, vb_ref, db_ref, of_ref, ob_ref, sf_ref, sb_ref):
    @pl.when(pl.program_id(1) == 0)
    def _():
        sf_ref[...] = jnp.zeros_like(sf_ref)
        sb_ref[...] = jnp.zeros_like(sb_ref)

    _gla_direction(gf_ref, vf_ref, df_ref[0, :, :GLA_QK], of_ref, sf_ref, True)
    _gla_direction(gb_ref, vb_ref, db_ref[0, :, GLA_QK:], ob_ref, sb_ref, False)


def _gla(gf, gb, gv, dec, chunk):
    b, s, _ = gv.shape
    n = s // chunk
    fwd = lambda w: pl.BlockSpec((1, chunk, w), lambda bi, ci: (bi, ci, 0))
    bwd = lambda w: pl.BlockSpec((1, chunk, w), lambda bi, ci: (bi, n - 1 - ci, 0))
    dec_fwd = pl.BlockSpec((1, 1, 2 * GLA_QK), lambda bi, ci: (bi * n + ci, 0, 0))
    dec_bwd = pl.BlockSpec((1, 1, 2 * GLA_QK), lambda bi, ci: (bi * n + n - 1 - ci, 0, 0))
    out = jax.ShapeDtypeStruct((b, s, GLA_WIDTH), BF16)
    state = pltpu.VMEM((GLA_HEADS, GLA_DV, LANES), F32)
    return pl.pallas_call(
        _gla_kernel,
        grid=(b, n),
        in_specs=[fwd(4 * GLA_QK), fwd(GLA_WIDTH), dec_fwd, bwd(4 * GLA_QK), bwd(GLA_WIDTH), dec_bwd],
        out_specs=[fwd(GLA_WIDTH), bwd(GLA_WIDTH)],
        out_shape=[out, out],
        scratch_shapes=[state, state],
        compiler_params=pltpu.CompilerParams(
            dimension_semantics=("arbitrary", "arbitrary"), vmem_limit_bytes=VMEM_LIMIT),
        name="gla_scan",
    )(gf, gv, dec, gb, gv, dec)


def _back_kernel(h_ref, ym_ref, of_ref, ob_ref, sgg_ref, on_ref, wout_ref, pn_ref, wpg_ref, p_ref, wpp_ref,
                 fn_ref, o_ref, *, last):
    og = of_ref[...].astype(F32) + ob_ref[...].astype(F32)
    sgg = sgg_ref[...].astype(F32)
    on = on_ref[...]
    parts = [ym_ref[...]]
    for hd in range(GLA_HEADS):
        g = slice(hd * GLA_DV, (hd + 1) * GLA_DV)
        parts.append((_rms(og[:, g], on) * sgg[:, g]).astype(BF16))
    h1 = h_ref[...] + _dot(jnp.concatenate(parts, axis=1), wout_ref[...])
    gate = jax.nn.sigmoid(_dot(_rms(h1, pn_ref[...]).astype(BF16), wpg_ref[...]))
    h2 = h1 + gate * _dot(p_ref[0].astype(BF16), wpp_ref[...])
    o_ref[...] = _rms(h2, fn_ref[...]) if last else h2


def _back(h, y_mla, o_f, o_b, sgg, out_norm, w_out, ple_norm, w_pg, p, layer, w_pp, final_norm, tm, last):
    tokens, d = h.shape
    row = lambda n: pl.BlockSpec((tm, n), lambda i: (i, 0))
    full = lambda a: pl.BlockSpec(a.shape, lambda i: (0, 0))
    p_spec = pl.BlockSpec((1, tm, p.shape[2]), lambda i: (layer, i, 0))
    on = out_norm.reshape(1, -1)
    pn = ple_norm.reshape(1, -1)
    fn = final_norm.reshape(1, -1)
    w_out = w_out.astype(BF16)
    w_pg = w_pg.astype(BF16)
    w_pp = w_pp.astype(BF16)
    return pl.pallas_call(
        functools.partial(_back_kernel, last=last),
        grid=(tokens // tm,),
        in_specs=[row(d), row(MLA_WIDTH), row(GLA_WIDTH), row(GLA_WIDTH), row(GLA_WIDTH), full(on),
                  full(w_out), full(pn), full(w_pg), p_spec, full(w_pp), full(fn)],
        out_specs=row(d),
        out_shape=jax.ShapeDtypeStruct((tokens, d), F32),
        compiler_params=pltpu.CompilerParams(dimension_semantics=("arbitrary",), vmem_limit_bytes=VMEM_LIMIT),
        name="back",
    )(h, y_mla, o_f, o_b, sgg, on, w_out, pn, w_pg, p, w_pp, fn)


def kernel(x, p, positions, ln_mix, w_in, mla_q_norm, w_uq, mla_kv_norm, w_ukv, gla_w_gate_fwd, gla_b_gate_fwd,
           gla_w_gate_bwd, gla_b_gate_bwd, gla_out_norm, w_out, ple_norm, w_ple_gate, w_ple_proj, final_norm):
    batch, seq, d = x.shape
    depth = w_in.shape[0]
    tokens = batch * seq
    t = _tiles(batch, seq)
    tabs = _rope_tables(positions, t["rope_rows"])
    h = x.reshape(tokens, d)
    p_all = p.reshape(depth, tokens, p.shape[-1])
    seq3 = lambda a: a.reshape(batch, seq, a.shape[-1])
    for i in range(depth):
        weights = _front_weights(w_in[i], w_uq[i], w_ukv[i], gla_w_gate_fwd[i], gla_b_gate_fwd[i],
                                 gla_w_gate_bwd[i], gla_b_gate_bwd[i])
        q, k, v, sga, gf, gb, dec, gv, sgg = _front(
            h, ln_mix[i], weights, mla_q_norm[i], mla_kv_norm[i], tabs, t["row_tile"], t["gla_chunk"])
        y_mla = _attention(seq3(q), seq3(k), seq3(v), seq3(sga), t["q_tile"], t["kv_tile"])
        o_f, o_b = _gla(seq3(gf), seq3(gb), seq3(gv), dec, t["gla_chunk"])
        h = _back(h, y_mla.reshape(tokens, -1), o_f.reshape(tokens, -1), o_b.reshape(tokens, -1), sgg,
                  gla_out_norm[i], w_out[i], ple_norm[i], w_ple_gate[i], p_all, i,
                  w_ple_proj[i], final_norm, t["row_tile"], last=(i == depth - 1))
    return h.reshape(batch, seq, d)
```

```python
import functools
import math

import jax
import jax.numpy as jnp
from jax import lax
from jax.experimental import pallas as pl
from jax.experimental.pallas import tpu as pltpu

EPS = 1e-6
MLA_HEADS = 4
MLA_Q_RANK = 384
MLA_KV_RANK = 256
MLA_NOPE = 128
MLA_ROPE = 64
MLA_V = 128
MLA_WIDTH = MLA_HEADS * MLA_V
ROPE_THETA = 10000.0
GLA_HEADS = 4
GLA_DK = 64
GLA_DV = 128
GLA_WIDTH = GLA_HEADS * GLA_DV
GLA_QK = GLA_HEADS * GLA_DK
GLA_GATE_RANK = 16
GLA_TAU = 16.0

LANES = 128
QK_PAD = 2 * LANES
VMEM_LIMIT = 48 * 1024 * 1024

Q_SCALE = (MLA_NOPE + MLA_ROPE) ** -0.5 * math.log2(math.e)

F32 = jnp.float32
BF16 = jnp.bfloat16


def _tiles(batch, seq):
    tokens = batch * seq
    row_tile = math.gcd(tokens, 512)
    return dict(
        row_tile=row_tile,
        back_tile=math.gcd(tokens, 1024),
        q_tile=math.gcd(seq, 1024),
        kv_tile=math.gcd(seq, 4096),
        gla_chunk=math.gcd(seq, 256),
        rope_rows=math.gcd(tokens // 4, 1024),
    )


def _dot(a, b):
    return jnp.dot(a, b, preferred_element_type=F32)


def _dot_nt(a, b):
    return lax.dot_general(a, b, (((1,), (1,)), ((), ())), preferred_element_type=F32)


def _dot_tn(a, b):
    return lax.dot_general(a, b, (((0,), (0,)), ((), ())), preferred_element_type=F32)


def _rms(x, g):
    return x * lax.rsqrt(jnp.mean(x * x, axis=-1, keepdims=True) + EPS) * g


def _silu(x):
    return x * jax.nn.sigmoid(x)


def _log2_sigmoid(x):
    log2e = math.log2(math.e)
    return jnp.minimum(x, 0.0) * log2e - jnp.log2(1.0 + jnp.exp2(jnp.abs(x) * -log2e))


def _split2(x):
    hi = x.astype(BF16)
    lo = (x - hi.astype(F32)).astype(BF16)
    return hi, lo


def _rope_table_kernel(pos_ref, inv_ref, tq_ref, tkc_ref, tks_ref):
    half = MLA_ROPE // 2
    ang = pos_ref[...] * inv_ref[...]
    cos = jnp.cos(ang)
    sin = jnp.sin(ang)
    lane = lax.broadcasted_iota(jnp.int32, cos.shape, 1)
    for i in range(LANES // half):
        place = lambda x, dst: x if dst == i else pltpu.roll(x, ((dst - i) * half) % LANES, axis=1)
        base = jnp.where(lane < half, place(cos, 0),
                         jnp.where(lane < 2 * half, place(cos, 1),
                                   jnp.where(lane < 3 * half, -place(sin, 2), place(sin, 3))))
        tq_ref[i] = base * Q_SCALE
        tkc_ref[i] = jnp.where(lane < 2 * half, base, 0.0)
        tks_ref[i] = jnp.where(lane < 2 * half, pltpu.roll(base, 2 * half, axis=1), 0.0)


def _rope_tables(positions, rows):
    half = MLA_ROPE // 2
    per_row = LANES // half
    tokens = positions.size
    n = tokens // per_row
    inv = ROPE_THETA ** (-jnp.arange(half, dtype=F32) / half)
    pos = jnp.broadcast_to(positions.reshape(per_row, n, 1).astype(F32), (per_row, n, half))
    pos = pos.transpose(1, 0, 2).reshape(n, LANES)
    inv4 = jnp.tile(inv, per_row).reshape(1, LANES)
    out_spec = pl.BlockSpec((per_row, rows, LANES), lambda i: (0, i, 0))
    tabs = pl.pallas_call(
        _rope_table_kernel,
        grid=(n // rows,),
        in_specs=[pl.BlockSpec((rows, LANES), lambda i: (i, 0)), pl.BlockSpec((1, LANES), lambda i: (0, 0))],
        out_specs=[out_spec] * 3,
        out_shape=[jax.ShapeDtypeStruct((per_row, n, LANES), F32)] * 3,
        name="rope_tables",
    )(pos, inv4)
    return tuple(t.reshape(tokens, LANES) for t in tabs)


_C_CQ = 0
_C_KRLR = _C_CQ + MLA_Q_RANK
_C_CKV = _C_KRLR + LANES
_C_GA = _C_CKV + MLA_KV_RANK
_C_GQ = _C_GA + MLA_WIDTH
_C_GK = _C_GQ + GLA_QK
_C_GV = _C_GK + GLA_QK
_C_GG = _C_GV + GLA_WIDTH
_C_END = _C_GG + GLA_WIDTH
_LR_LANE = MLA_ROPE


def _front_kernel(h_ref, ln_ref, win_ref, qn_ref, wqn_ref, wqr_ref, kvn_ref, wk_ref, wv_ref,
                  wg_ref, bg_ref, tq_ref, tkc_ref, tks_ref,
                  q_ref, k_ref, v_ref, sga_ref, gf_ref, gb_ref, dec_ref, gv_ref, sgg_ref, *, chunk):
    xn = _rms(h_ref[...], ln_ref[...]).astype(BF16)

    def proj(c0, c1):
        return _dot(xn, win_ref[:, c0:c1])

    cq_krlr = proj(_C_CQ, _C_CKV)
    krlr = cq_krlr[:, _C_KRLR:_C_CKV]

    cqn = _rms(cq_krlr[:, :_C_KRLR], qn_ref[...]).astype(BF16)
    q_nope = _dot(cqn, wqn_ref[...]) * Q_SCALE
    q_rope = _dot(cqn, wqr_ref[...])
    tq = tq_ref[...]
    for hd in range(MLA_HEADS):
        g = slice(hd * LANES, (hd + 1) * LANES)
        q_ref[:, hd * QK_PAD:hd * QK_PAD + LANES] = q_nope[:, g].astype(BF16)
        qr = q_rope[:, g] * tq
        q_ref[:, hd * QK_PAD + LANES:(hd + 1) * QK_PAD] = (qr + pltpu.roll(qr, MLA_ROPE, axis=1)).astype(BF16)

    ckvn = _rms(proj(_C_CKV, _C_GA), kvn_ref[...]).astype(BF16)
    k_nope = _dot(ckvn, wk_ref[...])
    v_ref[...] = _dot(ckvn, wv_ref[...]).astype(BF16)
    half = MLA_ROPE // 2
    lane = lax.broadcasted_iota(jnp.int32, krlr.shape, 1)
    kr_sw = jnp.where(lane < half, pltpu.roll(krlr, LANES - half, axis=1), pltpu.roll(krlr, half, axis=1))
    kr = (krlr * tkc_ref[...] + kr_sw * tks_ref[...]).astype(BF16)
    for hd in range(MLA_HEADS):
        k_ref[:, hd * QK_PAD:hd * QK_PAD + LANES] = k_nope[:, hd * LANES:(hd + 1) * LANES].astype(BF16)
        k_ref[:, hd * QK_PAD + LANES:(hd + 1) * QK_PAD] = kr

    sga_ref[...] = _silu(proj(_C_GA, _C_GQ)).astype(BF16)
    sgg_ref[...] = _silu(proj(_C_GG, _C_END)).astype(BF16)

    gq = proj(_C_GQ, _C_GK) * GLA_DK ** -0.5
    gk = proj(_C_GK, _C_GV)
    gv_ref[...] = proj(_C_GV, _C_GG).astype(BF16)
    la = _log2_sigmoid(_dot(krlr.astype(BF16), wg_ref[...]) + bg_ref[...]) * (1.0 / GLA_TAU)
    row = lax.broadcasted_iota(jnp.int32, (chunk, chunk), 0)
    col = lax.broadcasted_iota(jnp.int32, (chunk, chunk), 1)
    for d, out_ref in enumerate((gf_ref, gb_ref)):
        forward = d == 0
        tri = jnp.where((col <= row) if forward else (col >= row), 1.0, 0.0).astype(BF16)
        for ci in range(la.shape[0] // chunk):
            r = slice(ci * chunk, (ci + 1) * chunk)
            hi, lo = _split2(la[r, d * GLA_QK:(d + 1) * GLA_QK])
            b = _dot(tri, jnp.concatenate([hi, lo], axis=1))
            b = b[:, :GLA_QK] + b[:, GLA_QK:]
            mid = chunk // 2 - 1 if forward else chunk // 2
            end = chunk - 1 if forward else 0
            b_mid = b[mid:mid + 1, :]
            b_end = b[end:end + 1, :]
            q_in = gq[r] * jnp.exp2(b - b_mid)
            k_in = gk[r] * jnp.exp2(b_mid - b)
            out_ref[r, 0 * GLA_QK:1 * GLA_QK] = q_in.astype(BF16)
            out_ref[r, 1 * GLA_QK:2 * GLA_QK] = k_in.astype(BF16)
            out_ref[r, 2 * GLA_QK:3 * GLA_QK] = (q_in * jnp.exp2(b_mid)).astype(BF16)
            out_ref[r, 3 * GLA_QK:4 * GLA_QK] = (k_in * jnp.exp2(b_end - b_mid)).astype(BF16)
            dec_ref[ci, :, d * GLA_QK:(d + 1) * GLA_QK] = jnp.exp2(b_end)


def _front_weights(w_in, w_uq, w_ukv, wgf, bgf, wgb, bgb):
    depth, d, _ = w_in.shape
    c = 0
    parts = {}
    for name, size in (("cq", MLA_Q_RANK), ("ckv", MLA_KV_RANK), ("kr", MLA_ROPE), ("ga", MLA_WIDTH),
                       ("gq", GLA_QK), ("gk", GLA_QK), ("gv", GLA_WIDTH), ("lrf", GLA_GATE_RANK),
                       ("lrb", GLA_GATE_RANK), ("gg", GLA_WIDTH)):
        parts[name] = w_in[..., c:c + size]
        c += size
    half = MLA_ROPE // 2
    swap = lambda w: jnp.concatenate([w[..., half:], w[..., :half]], axis=-1)
    zeros = lambda n: jnp.zeros((depth, d, n), w_in.dtype)
    win = jnp.concatenate([
        parts["cq"],
        parts["kr"], parts["lrf"], parts["lrb"], zeros(LANES - MLA_ROPE - 2 * GLA_GATE_RANK),
        parts["ckv"], parts["ga"], parts["gq"], parts["gk"], parts["gv"], parts["gg"]], axis=-1).astype(BF16)

    wq = w_uq.reshape(depth, MLA_Q_RANK, MLA_HEADS, MLA_NOPE + MLA_ROPE)
    wqn = wq[..., :MLA_NOPE].reshape(depth, MLA_Q_RANK, MLA_HEADS * MLA_NOPE).astype(BF16)
    wr = wq[..., MLA_NOPE:]
    wqr = jnp.concatenate([wr, swap(wr)], axis=-1).reshape(depth, MLA_Q_RANK, MLA_HEADS * LANES).astype(BF16)

    wkv = w_ukv.reshape(depth, MLA_KV_RANK, MLA_HEADS, MLA_NOPE + MLA_V)
    wk = wkv[..., :MLA_NOPE].reshape(depth, MLA_KV_RANK, MLA_HEADS * MLA_NOPE).astype(BF16)
    wv = wkv[..., MLA_NOPE:].reshape(depth, MLA_KV_RANK, MLA_HEADS * MLA_V).astype(BF16)

    wg = jnp.zeros((depth, LANES, 2 * GLA_QK), F32)
    wg = wg.at[:, _LR_LANE:_LR_LANE + GLA_GATE_RANK, :GLA_QK].set(wgf)
    wg = wg.at[:, _LR_LANE + GLA_GATE_RANK:_LR_LANE + 2 * GLA_GATE_RANK, GLA_QK:].set(wgb)
    bg = jnp.concatenate([bgf, bgb], axis=-1).reshape(depth, 1, 2 * GLA_QK)
    return win, wqn, wqr, wk, wv, wg.astype(BF16), bg


def _layer_spec(a, layer):
    return pl.BlockSpec((None,) + a.shape[1:], lambda i: (layer, 0, 0))


def _front(h, ln, weights, qn, kvn, tabs, layer, tm, chunk):
    tokens, d = h.shape
    assert tm % chunk == 0, (tm, chunk)
    win, wqn, wqr, wk, wv, wg, bg = weights
    tq, tkc, tks = tabs
    row = lambda n: pl.BlockSpec((tm, n), lambda i: (i, 0))
    full = lambda a: _layer_spec(a, layer)
    ln, qn, kvn = (a.reshape(a.shape[0], 1, a.shape[1]) for a in (ln, qn, kvn))
    outs = [
        (MLA_HEADS * QK_PAD, BF16),
        (MLA_HEADS * QK_PAD, BF16),
        (MLA_WIDTH, BF16),
        (MLA_WIDTH, BF16),
        (4 * GLA_QK, BF16),
        (4 * GLA_QK, BF16),
        None,
        (GLA_WIDTH, BF16),
        (GLA_WIDTH, BF16),
    ]
    dec_spec = pl.BlockSpec((tm // chunk, 1, 2 * GLA_QK), lambda i: (i, 0, 0))
    dec_shape = jax.ShapeDtypeStruct((tokens // chunk, 1, 2 * GLA_QK), F32)
    return pl.pallas_call(
        functools.partial(_front_kernel, chunk=chunk),
        grid=(tokens // tm,),
        in_specs=[row(d), full(ln), full(win), full(qn), full(wqn), full(wqr), full(kvn),
                  full(wk), full(wv), full(wg), full(bg), row(LANES), row(LANES), row(LANES)],
        out_specs=[dec_spec if o is None else row(o[0]) for o in outs],
        out_shape=[dec_shape if o is None else jax.ShapeDtypeStruct((tokens, o[0]), o[1]) for o in outs],
        compiler_params=pltpu.CompilerParams(dimension_semantics=("arbitrary",), vmem_limit_bytes=VMEM_LIMIT),
        name="front",
    )(h, ln, win, qn, wqn, wqr, kvn, wk, wv, wg, bg, tq, tkc, tks)


def _attn_kernel(q_ref, k_ref, v_ref, g_ref, o_ref, s_ref, macc_ref, acc_ref, vext_ref, *, kv_tile):
    @pl.when(pl.program_id(2) == 0)
    def _():
        vext_ref[:, :MLA_V] = v_ref[0]
        vext_ref[:, MLA_V:] = jnp.ones((v_ref.shape[1], MLA_V), BF16)

    tq = q_ref.shape[1]
    n_kv = k_ref.shape[1] // kv_tile
    groups = kv_tile // LANES
    n_dyn = jnp.minimum(pl.program_id(0) + n_kv, n_kv)
    macc_ref[...] = jnp.full((tq, LANES), -jnp.inf, F32)
    acc_ref[...] = jnp.zeros((tq, 2 * MLA_V), F32)

    @pl.loop(0, n_dyn)
    def _(j):
        rows = pl.ds(pl.multiple_of(j * kv_tile, kv_tile), kv_tile)
        s = _dot_nt(q_ref[0], k_ref[0, rows, :])
        s_ref[:, rows] = s
        m_acc = macc_ref[...]
        for c in range(groups):
            m_acc = jnp.maximum(m_acc, s[:, c * LANES:(c + 1) * LANES])
        macc_ref[...] = m_acc

    m_b = jnp.broadcast_to(jnp.max(macc_ref[...], axis=-1, keepdims=True), (tq, LANES))

    @pl.loop(0, n_dyn)
    def _(j):
        rows = pl.ds(pl.multiple_of(j * kv_tile, kv_tile), kv_tile)
        s_blk = s_ref[:, rows]
        cols = [jnp.exp2((s_blk[:, c * LANES:(c + 1) * LANES] - m_b).astype(BF16)) for c in range(groups)]
        acc_ref[...] += _dot(jnp.concatenate(cols, axis=1), vext_ref[rows, :])

    acc = acc_ref[...]
    o_ref[0] = (acc[:, :MLA_V] / acc[:, MLA_V:] * g_ref[0].astype(F32)).astype(o_ref.dtype)


def _attention(q, k, v, sgate, tq, kv_tile):
    b, s, _ = q.shape
    return pl.pallas_call(
        functools.partial(_attn_kernel, kv_tile=kv_tile),
        grid=(b, MLA_HEADS, s // tq),
        in_specs=[
            pl.BlockSpec((1, tq, QK_PAD), lambda bi, hi, qi: (bi, qi, hi)),
            pl.BlockSpec((1, s, QK_PAD), lambda bi, hi, qi: (bi, 0, hi)),
            pl.BlockSpec((1, s, MLA_V), lambda bi, hi, qi: (bi, 0, hi)),
            pl.BlockSpec((1, tq, MLA_V), lambda bi, hi, qi: (bi, qi, hi)),
        ],
        out_specs=pl.BlockSpec((1, tq, MLA_V), lambda bi, hi, qi: (bi, qi, hi)),
        out_shape=jax.ShapeDtypeStruct((b, s, MLA_WIDTH), BF16),
        scratch_shapes=[pltpu.VMEM((tq, s), F32), pltpu.VMEM((tq, LANES), F32), pltpu.VMEM((tq, 2 * MLA_V), F32),
                        pltpu.VMEM((s, 2 * MLA_V), BF16)],
        compiler_params=pltpu.CompilerParams(
            dimension_semantics=("arbitrary", "arbitrary", "arbitrary"), vmem_limit_bytes=VMEM_LIMIT),
        name="mla_attention",
    )(q, k, v, sgate)


GLA_CHUNKS_PER_STEP = 8


def _gla_direction(g_ref, v_ref, dec_ref, o_ref, state_ref, forward, chunk):
    c = chunk
    n_sub = g_ref.shape[1] // c
    row = lax.broadcasted_iota(jnp.int32, (c, c), 0)
    col = lax.broadcasted_iota(jnp.int32, (c, c), 1)
    tri = (col <= row) if forward else (col >= row)
    lane_head = lax.broadcasted_iota(jnp.int32, (c, LANES), 1) // GLA_DK
    zero = jnp.zeros((c, LANES), BF16)
    lanes0 = 0 if forward else GLA_QK
    for sub in (range(n_sub) if forward else reversed(range(n_sub))):
        r = slice(sub * c, (sub + 1) * c)
        for hd in range(GLA_HEADS):
            pair = (hd // 2) * LANES
            part = lambda i: g_ref[0, r, i * GLA_QK + pair:i * GLA_QK + pair + LANES]
            mine = lane_head == (hd % 2)
            vh = v_ref[0, r, hd * GLA_DV:(hd + 1) * GLA_DV]
            a = _dot_nt(part(0), jnp.where(mine, part(1), zero))
            a = jnp.where(tri, a, 0.0).astype(BF16)
            st = state_ref[hd]
            o = _dot(a, vh) + _dot_nt(part(2), st.astype(BF16))
            o_ref[0, r, hd * GLA_DV:(hd + 1) * GLA_DV] = o.astype(o_ref.dtype)
            decay = dec_ref[sub, :, lanes0 + pair:lanes0 + pair + LANES]
            state_ref[hd] = st * decay + _dot_tn(vh, jnp.where(mine, part(3), zero))


def _gla_kernel(gf_ref, vf_ref, df_ref, gb_ref, vb_ref, db_ref, of_ref, ob_ref, sf_ref, sb_ref, *, chunk):
    @pl.when(pl.program_id(1) == 0)
    def _():
        sf_ref[...] = jnp.zeros_like(sf_ref)
        sb_ref[...] = jnp.zeros_like(sb_ref)

    _gla_direction(gf_ref, vf_ref, df_ref, of_ref, sf_ref, True, chunk)
    _gla_direction(gb_ref, vb_ref, db_ref, ob_ref, sb_ref, False, chunk)


def _gla(gf, gb, gv, dec, chunk):
    b, s, _ = gv.shape
    per = math.gcd(s // chunk, GLA_CHUNKS_PER_STEP)
    rows = per * chunk
    n = s // rows
    fwd = lambda w: pl.BlockSpec((1, rows, w), lambda bi, ci: (bi, ci, 0))
    bwd = lambda w: pl.BlockSpec((1, rows, w), lambda bi, ci: (bi, n - 1 - ci, 0))
    dec_fwd = pl.BlockSpec((per, 1, 2 * GLA_QK), lambda bi, ci: (bi * n + ci, 0, 0))
    dec_bwd = pl.BlockSpec((per, 1, 2 * GLA_QK), lambda bi, ci: (bi * n + n - 1 - ci, 0, 0))
    out = jax.ShapeDtypeStruct((b, s, GLA_WIDTH), BF16)
    state = pltpu.VMEM((GLA_HEADS, GLA_DV, LANES), F32)
    return pl.pallas_call(
        functools.partial(_gla_kernel, chunk=chunk),
        grid=(b, n),
        in_specs=[fwd(4 * GLA_QK), fwd(GLA_WIDTH), dec_fwd, bwd(4 * GLA_QK), bwd(GLA_WIDTH), dec_bwd],
        out_specs=[fwd(GLA_WIDTH), bwd(GLA_WIDTH)],
        out_shape=[out, out],
        scratch_shapes=[state, state],
        compiler_params=pltpu.CompilerParams(
            dimension_semantics=("arbitrary", "arbitrary"), vmem_limit_bytes=VMEM_LIMIT),
        name="gla_scan",
    )(gf, gv, dec, gb, gv, dec)


def _back_kernel(h_ref, ym_ref, of_ref, ob_ref, sgg_ref, on_ref, wout_ref, pn_ref, wpg_ref, p_ref, wpp_ref,
                 fn_ref, o_ref, *, last):
    og = of_ref[...].astype(F32) + ob_ref[...].astype(F32)
    sgg = sgg_ref[...].astype(F32)
    on = on_ref[...]
    parts = [ym_ref[...]]
    for hd in range(GLA_HEADS):
        g = slice(hd * GLA_DV, (hd + 1) * GLA_DV)
        parts.append((_rms(og[:, g], on) * sgg[:, g]).astype(BF16))
    h1 = h_ref[...] + _dot(jnp.concatenate(parts, axis=1), wout_ref[...])
    gate = jax.nn.sigmoid(_dot(_rms(h1, pn_ref[...]).astype(BF16), wpg_ref[...]))
    h2 = h1 + gate * _dot(p_ref[0].astype(BF16), wpp_ref[...])
    o_ref[...] = _rms(h2, fn_ref[...]) if last else h2


def _back(h, y_mla, o_f, o_b, sgg, out_norm, w_out, ple_norm, w_pg, p, layer, w_pp, final_norm, tm, last):
    tokens, d = h.shape
    row = lambda n: pl.BlockSpec((tm, n), lambda i: (i, 0))
    full = lambda a: _layer_spec(a, layer)
    p_spec = pl.BlockSpec((1, tm, p.shape[2]), lambda i: (layer, i, 0))
    on, pn = (a.reshape(a.shape[0], 1, a.shape[1]) for a in (out_norm, ple_norm))
    fn = final_norm.reshape(1, -1)
    return pl.pallas_call(
        functools.partial(_back_kernel, last=last),
        grid=(tokens // tm,),
        in_specs=[row(d), row(MLA_WIDTH), row(GLA_WIDTH), row(GLA_WIDTH), row(GLA_WIDTH), full(on),
                  full(w_out), full(pn), full(w_pg), p_spec, full(w_pp),
                  pl.BlockSpec(fn.shape, lambda i: (0, 0))],
        out_specs=row(d),
        out_shape=jax.ShapeDtypeStruct((tokens, d), F32),
        compiler_params=pltpu.CompilerParams(dimension_semantics=("arbitrary",), vmem_limit_bytes=VMEM_LIMIT),
        name="back",
    )(h, y_mla, o_f, o_b, sgg, on, w_out, pn, w_pg, p, w_pp, fn)


def kernel(x, p, positions, ln_mix, w_in, mla_q_norm, w_uq, mla_kv_norm, w_ukv, gla_w_gate_fwd, gla_b_gate_fwd,
           gla_w_gate_bwd, gla_b_gate_bwd, gla_out_norm, w_out, ple_norm, w_ple_gate, w_ple_proj, final_norm):
    batch, seq, d = x.shape
    depth = w_in.shape[0]
    tokens = batch * seq
    t = _tiles(batch, seq)
    tabs = _rope_tables(positions, t["rope_rows"])
    h = x.reshape(tokens, d)
    p_all = p.reshape(depth, tokens, p.shape[-1])
    seq3 = lambda a: a.reshape(batch, seq, a.shape[-1])
    weights = _front_weights(w_in, w_uq, w_ukv, gla_w_gate_fwd, gla_b_gate_fwd, gla_w_gate_bwd, gla_b_gate_bwd)
    w_out, w_ple_gate, w_ple_proj = (w.astype(BF16) for w in (w_out, w_ple_gate, w_ple_proj))
    for i in range(depth):
        q, k, v, sga, gf, gb, dec, gv, sgg = _front(
            h, ln_mix, weights, mla_q_norm, mla_kv_norm, tabs, i, t["row_tile"], t["gla_chunk"])
        y_mla = _attention(seq3(q), seq3(k), seq3(v), seq3(sga), t["q_tile"], t["kv_tile"])
        o_f, o_b = _gla(seq3(gf), seq3(gb), seq3(gv), dec, t["gla_chunk"])
        h = _back(h, y_mla.reshape(tokens, -1), o_f.reshape(tokens, -1), o_b.reshape(tokens, -1), sgg,
                  gla_out_norm, w_out, ple_norm, w_ple_gate, p_all, i,
                  w_ple_proj, final_norm, t["back_tile"], last=(i == depth - 1))
    return h.reshape(batch, seq, d)
```

```python
import functools
import math

import jax
import jax.numpy as jnp
from jax import lax
from jax.experimental import pallas as pl
from jax.experimental.pallas import tpu as pltpu

EPS = 1e-6
MLA_HEADS = 4
MLA_Q_RANK = 384
MLA_KV_RANK = 256
MLA_NOPE = 128
MLA_ROPE = 64
MLA_V = 128
MLA_WIDTH = MLA_HEADS * MLA_V
ROPE_THETA = 10000.0
GLA_HEADS = 4
GLA_DK = 64
GLA_DV = 128
GLA_WIDTH = GLA_HEADS * GLA_DV
GLA_QK = GLA_HEADS * GLA_DK
GLA_GATE_RANK = 16
GLA_TAU = 16.0

LANES = 128
QK_PAD = 2 * LANES
VMEM_LIMIT = 48 * 1024 * 1024

Q_SCALE = (MLA_NOPE + MLA_ROPE) ** -0.5 * math.log2(math.e)

F32 = jnp.float32
BF16 = jnp.bfloat16


def _tiles(batch, seq):
    tokens = batch * seq
    row_tile = math.gcd(tokens, 512)
    return dict(
        row_tile=row_tile,
        back_tile=math.gcd(tokens, 1024),
        q_tile=math.gcd(seq, 1024),
        gla_chunk=math.gcd(seq, 256),
        rope_rows=math.gcd(tokens // 4, 1024),
    )


def _dot(a, b):
    return jnp.dot(a, b, preferred_element_type=F32)


def _dot_nt(a, b):
    return lax.dot_general(a, b, (((1,), (1,)), ((), ())), preferred_element_type=F32)


def _dot_tn(a, b):
    return lax.dot_general(a, b, (((0,), (0,)), ((), ())), preferred_element_type=F32)


def _rms(x, g):
    return x * lax.rsqrt(jnp.mean(x * x, axis=-1, keepdims=True) + EPS) * g


def _silu(x):
    return x * jax.nn.sigmoid(x)


def _log2_sigmoid(x):
    log2e = math.log2(math.e)
    return jnp.minimum(x, 0.0) * log2e - jnp.log2(1.0 + jnp.exp2(jnp.abs(x) * -log2e))


def _split2(x):
    hi = x.astype(BF16)
    lo = (x - hi.astype(F32)).astype(BF16)
    return hi, lo


def _rope_table_kernel(pos_ref, inv_ref, tq_ref, tkc_ref, tks_ref):
    half = MLA_ROPE // 2
    ang = pos_ref[...] * inv_ref[...]
    cos = jnp.cos(ang)
    sin = jnp.sin(ang)
    lane = lax.broadcasted_iota(jnp.int32, cos.shape, 1)
    for i in range(LANES // half):
        place = lambda x, dst: x if dst == i else pltpu.roll(x, ((dst - i) * half) % LANES, axis=1)
        base = jnp.where(lane < half, place(cos, 0),
                         jnp.where(lane < 2 * half, place(cos, 1),
                                   jnp.where(lane < 3 * half, -place(sin, 2), place(sin, 3))))
        tq_ref[i] = base * Q_SCALE
        tkc_ref[i] = jnp.where(lane < 2 * half, base, 0.0)
        tks_ref[i] = jnp.where(lane < 2 * half, pltpu.roll(base, 2 * half, axis=1), 0.0)


def _rope_tables(positions, rows):
    half = MLA_ROPE // 2
    per_row = LANES // half
    tokens = positions.size
    n = tokens // per_row
    inv = ROPE_THETA ** (-jnp.arange(half, dtype=F32) / half)
    pos = jnp.broadcast_to(positions.reshape(per_row, n, 1).astype(F32), (per_row, n, half))
    pos = pos.transpose(1, 0, 2).reshape(n, LANES)
    inv4 = jnp.tile(inv, per_row).reshape(1, LANES)
    out_spec = pl.BlockSpec((per_row, rows, LANES), lambda i: (0, i, 0))
    tabs = pl.pallas_call(
        _rope_table_kernel,
        grid=(n // rows,),
        in_specs=[pl.BlockSpec((rows, LANES), lambda i: (i, 0)), pl.BlockSpec((1, LANES), lambda i: (0, 0))],
        out_specs=[out_spec] * 3,
        out_shape=[jax.ShapeDtypeStruct((per_row, n, LANES), F32)] * 3,
        name="rope_tables",
    )(pos, inv4)
    return tuple(t.reshape(tokens, LANES) for t in tabs)


_C_CQ = 0
_C_KRLR = _C_CQ + MLA_Q_RANK
_C_CKV = _C_KRLR + LANES
_C_GA = _C_CKV + MLA_KV_RANK
_C_GQ = _C_GA + MLA_WIDTH
_C_GK = _C_GQ + GLA_QK
_C_GV = _C_GK + GLA_QK
_C_GG = _C_GV + GLA_WIDTH
_C_END = _C_GG + GLA_WIDTH
_LR_LANE = MLA_ROPE


def _front_kernel(h_ref, ln_ref, win_ref, qn_ref, wq_ref, kvn_ref, wkv_ref,
                  wg_ref, bg_ref, tq_ref, tkc_ref, tks_ref,
                  q_ref, k_ref, v_ref, sga_ref, gf_ref, gb_ref, dec_ref, gv_ref, sgg_ref, *, chunk):
    xn = _rms(h_ref[...], ln_ref[...]).astype(BF16)
    u = _dot(xn, win_ref[...])

    def proj(c0, c1):
        return u[:, c0:c1]

    cq_krlr = proj(_C_CQ, _C_CKV)
    krlr = cq_krlr[:, _C_KRLR:_C_CKV]

    cqn = _rms(cq_krlr[:, :_C_KRLR], qn_ref[...]).astype(BF16)
    q_all = _dot(cqn, wq_ref[...])
    q_nope = q_all[:, :MLA_HEADS * MLA_NOPE] * Q_SCALE
    q_rope = q_all[:, MLA_HEADS * MLA_NOPE:]
    tq = tq_ref[...]
    for hd in range(MLA_HEADS):
        g = slice(hd * LANES, (hd + 1) * LANES)
        q_ref[:, hd * QK_PAD:hd * QK_PAD + LANES] = q_nope[:, g].astype(BF16)
        qr = q_rope[:, g] * tq
        q_ref[:, hd * QK_PAD + LANES:(hd + 1) * QK_PAD] = (qr + pltpu.roll(qr, MLA_ROPE, axis=1)).astype(BF16)

    ckvn = _rms(proj(_C_CKV, _C_GA), kvn_ref[...]).astype(BF16)
    kv = _dot(ckvn, wkv_ref[...])
    k_nope = kv[:, :MLA_HEADS * MLA_NOPE]
    v_ref[...] = kv[:, MLA_HEADS * MLA_NOPE:].astype(BF16)
    half = MLA_ROPE // 2
    lane = lax.broadcasted_iota(jnp.int32, krlr.shape, 1)
    kr_sw = jnp.where(lane < half, pltpu.roll(krlr, LANES - half, axis=1), pltpu.roll(krlr, half, axis=1))
    kr = (krlr * tkc_ref[...] + kr_sw * tks_ref[...]).astype(BF16)
    for hd in range(MLA_HEADS):
        k_ref[:, hd * QK_PAD:hd * QK_PAD + LANES] = k_nope[:, hd * LANES:(hd + 1) * LANES].astype(BF16)
        k_ref[:, hd * QK_PAD + LANES:(hd + 1) * QK_PAD] = kr

    sga_ref[...] = _silu(proj(_C_GA, _C_GQ)).astype(BF16)
    sgg_ref[...] = _silu(proj(_C_GG, _C_END)).astype(BF16)

    gq = proj(_C_GQ, _C_GK) * GLA_DK ** -0.5
    gk = proj(_C_GK, _C_GV)
    gv_ref[...] = proj(_C_GV, _C_GG).astype(BF16)
    la = _log2_sigmoid(_dot(krlr.astype(BF16), wg_ref[...]) + bg_ref[...]) * (1.0 / GLA_TAU)
    row = lax.broadcasted_iota(jnp.int32, (chunk, chunk), 0)
    col = lax.broadcasted_iota(jnp.int32, (chunk, chunk), 1)
    for d, out_ref in enumerate((gf_ref, gb_ref)):
        forward = d == 0
        tri = jnp.where((col <= row) if forward else (col >= row), 1.0, 0.0).astype(BF16)
        for ci in range(la.shape[0] // chunk):
            r = slice(ci * chunk, (ci + 1) * chunk)
            hi, lo = _split2(la[r, d * GLA_QK:(d + 1) * GLA_QK])
            b = _dot(tri, jnp.concatenate([hi, lo], axis=1))
            b = b[:, :GLA_QK] + b[:, GLA_QK:]
            mid = chunk // 2 - 1 if forward else chunk // 2
            end = chunk - 1 if forward else 0
            b_mid = b[mid:mid + 1, :]
            b_end = b[end:end + 1, :]
            q_in = gq[r] * jnp.exp2(b - b_mid)
            k_in = gk[r] * jnp.exp2(b_mid - b)
            out_ref[r, 0 * GLA_QK:1 * GLA_QK] = q_in.astype(BF16)
            out_ref[r, 1 * GLA_QK:2 * GLA_QK] = k_in.astype(BF16)
            out_ref[r, 2 * GLA_QK:3 * GLA_QK] = (q_in * jnp.exp2(b_mid)).astype(BF16)
            out_ref[r, 3 * GLA_QK:4 * GLA_QK] = (k_in * jnp.exp2(b_end - b_mid)).astype(BF16)
            dec_ref[ci, :, d * GLA_QK:(d + 1) * GLA_QK] = jnp.exp2(b_end)


def _front_weights(w_in, w_uq, w_ukv, wgf, bgf, wgb, bgb):
    depth, d, _ = w_in.shape
    c = 0
    parts = {}
    for name, size in (("cq", MLA_Q_RANK), ("ckv", MLA_KV_RANK), ("kr", MLA_ROPE), ("ga", MLA_WIDTH),
                       ("gq", GLA_QK), ("gk", GLA_QK), ("gv", GLA_WIDTH), ("lrf", GLA_GATE_RANK),
                       ("lrb", GLA_GATE_RANK), ("gg", GLA_WIDTH)):
        parts[name] = w_in[..., c:c + size]
        c += size
    half = MLA_ROPE // 2
    swap = lambda w: jnp.concatenate([w[..., half:], w[..., :half]], axis=-1)
    zeros = lambda n: jnp.zeros((depth, d, n), w_in.dtype)
    win = jnp.concatenate([
        parts["cq"],
        parts["kr"], parts["lrf"], parts["lrb"], zeros(LANES - MLA_ROPE - 2 * GLA_GATE_RANK),
        parts["ckv"], parts["ga"], parts["gq"], parts["gk"], parts["gv"], parts["gg"]], axis=-1).astype(BF16)

    wq = w_uq.reshape(depth, MLA_Q_RANK, MLA_HEADS, MLA_NOPE + MLA_ROPE)
    wqn = wq[..., :MLA_NOPE].reshape(depth, MLA_Q_RANK, MLA_HEADS * MLA_NOPE).astype(BF16)
    wr = wq[..., MLA_NOPE:]
    wqr = jnp.concatenate([wr, swap(wr)], axis=-1).reshape(depth, MLA_Q_RANK, MLA_HEADS * LANES).astype(BF16)

    wkv = w_ukv.reshape(depth, MLA_KV_RANK, MLA_HEADS, MLA_NOPE + MLA_V)
    wk = wkv[..., :MLA_NOPE].reshape(depth, MLA_KV_RANK, MLA_HEADS * MLA_NOPE).astype(BF16)
    wv = wkv[..., MLA_NOPE:].reshape(depth, MLA_KV_RANK, MLA_HEADS * MLA_V).astype(BF16)

    wg = jnp.zeros((depth, LANES, 2 * GLA_QK), F32)
    wg = wg.at[:, _LR_LANE:_LR_LANE + GLA_GATE_RANK, :GLA_QK].set(wgf)
    wg = wg.at[:, _LR_LANE + GLA_GATE_RANK:_LR_LANE + 2 * GLA_GATE_RANK, GLA_QK:].set(wgb)
    bg = jnp.concatenate([bgf, bgb], axis=-1).reshape(depth, 1, 2 * GLA_QK)
    wq = jnp.concatenate([wqn, wqr], axis=-1)
    wkv = jnp.concatenate([wk, wv], axis=-1)
    return win, wq, wkv, wg.astype(BF16), bg


def _layer_spec(a, layer):
    return pl.BlockSpec((None,) + a.shape[1:], lambda i: (layer, 0, 0))


def _front(h, ln, weights, qn, kvn, tabs, layer, tm, chunk):
    tokens, d = h.shape
    assert tm % chunk == 0, (tm, chunk)
    win, wq, wkv, wg, bg = weights
    tq, tkc, tks = tabs
    row = lambda n: pl.BlockSpec((tm, n), lambda i: (i, 0))
    full = lambda a: _layer_spec(a, layer)
    ln, qn, kvn = (a.reshape(a.shape[0], 1, a.shape[1]) for a in (ln, qn, kvn))
    outs = [
        (MLA_HEADS * QK_PAD, BF16),
        (MLA_HEADS * QK_PAD, BF16),
        (MLA_WIDTH, BF16),
        (MLA_WIDTH, BF16),
        (4 * GLA_QK, BF16),
        (4 * GLA_QK, BF16),
        None,
        (GLA_WIDTH, BF16),
        (GLA_WIDTH, BF16),
    ]
    dec_spec = pl.BlockSpec((tm // chunk, 1, 2 * GLA_QK), lambda i: (i, 0, 0))
    dec_shape = jax.ShapeDtypeStruct((tokens // chunk, 1, 2 * GLA_QK), F32)
    return pl.pallas_call(
        functools.partial(_front_kernel, chunk=chunk),
        grid=(tokens // tm,),
        in_specs=[row(d), full(ln), full(win), full(qn), full(wq), full(kvn),
                  full(wkv), full(wg), full(bg), row(LANES), row(LANES), row(LANES)],
        out_specs=[dec_spec if o is None else row(o[0]) for o in outs],
        out_shape=[dec_shape if o is None else jax.ShapeDtypeStruct((tokens, o[0]), o[1]) for o in outs],
        compiler_params=pltpu.CompilerParams(dimension_semantics=("arbitrary",), vmem_limit_bytes=VMEM_LIMIT),
        name="front",
    )(h, ln, win, qn, wq, kvn, wkv, wg, bg, tq, tkc, tks)


def _attn_kernel(q_ref, k_ref, v_ref, g_ref, o_ref, s_ref, macc_ref, acc_ref, vext_ref):
    @pl.when(pl.program_id(2) == 0)
    def _():
        vext_ref[:, :MLA_V] = v_ref[0]
        vext_ref[:, MLA_V:] = jnp.ones((v_ref.shape[1], MLA_V), BF16)

    tq = q_ref.shape[1]
    groups = k_ref.shape[1] // LANES
    once = jnp.minimum(pl.program_id(0) + 1, 1)

    @pl.loop(0, once)
    def _(_):
        s = _dot_nt(q_ref[0], k_ref[0])
        s_ref[...] = s
        m_acc = s[:, :LANES]
        for c in range(1, groups):
            m_acc = jnp.maximum(m_acc, s[:, c * LANES:(c + 1) * LANES])
        macc_ref[...] = m_acc

    m_b = jnp.broadcast_to(jnp.max(macc_ref[...], axis=-1, keepdims=True), (tq, LANES))

    @pl.loop(0, once)
    def _(_):
        cols = [jnp.exp2((s_ref[:, c * LANES:(c + 1) * LANES] - m_b).astype(BF16)) for c in range(groups)]
        acc_ref[...] = _dot(jnp.concatenate(cols, axis=1), vext_ref[...])

    acc = acc_ref[...]
    o_ref[0] = (acc[:, :MLA_V] / acc[:, MLA_V:] * g_ref[0].astype(F32)).astype(o_ref.dtype)


def _attention(q, k, v, sgate, tq):
    b, s, _ = q.shape
    return pl.pallas_call(
        _attn_kernel,
        grid=(b, MLA_HEADS, s // tq),
        in_specs=[
            pl.BlockSpec((1, tq, QK_PAD), lambda bi, hi, qi: (bi, qi, hi)),
            pl.BlockSpec((1, s, QK_PAD), lambda bi, hi, qi: (bi, 0, hi)),
            pl.BlockSpec((1, s, MLA_V), lambda bi, hi, qi: (bi, 0, hi)),
            pl.BlockSpec((1, tq, MLA_V), lambda bi, hi, qi: (bi, qi, hi)),
        ],
        out_specs=pl.BlockSpec((1, tq, MLA_V), lambda bi, hi, qi: (bi, qi, hi)),
        out_shape=jax.ShapeDtypeStruct((b, s, MLA_WIDTH), BF16),
        scratch_shapes=[pltpu.VMEM((tq, s), F32), pltpu.VMEM((tq, LANES), F32), pltpu.VMEM((tq, 2 * MLA_V), F32),
                        pltpu.VMEM((s, 2 * MLA_V), BF16)],
        compiler_params=pltpu.CompilerParams(
            dimension_semantics=("arbitrary", "arbitrary", "arbitrary"), vmem_limit_bytes=VMEM_LIMIT),
        name="mla_attention",
    )(q, k, v, sgate)


GLA_CHUNKS_PER_STEP = 8


def _gla_direction(g_ref, v_ref, dec_ref, o_ref, state_ref, forward, chunk):
    c = chunk
    n_sub = g_ref.shape[1] // c
    row = lax.broadcasted_iota(jnp.int32, (c, c), 0)
    col = lax.broadcasted_iota(jnp.int32, (c, c), 1)
    tri = (col <= row) if forward else (col >= row)
    lane_head = lax.broadcasted_iota(jnp.int32, (c, LANES), 1) // GLA_DK
    zero = jnp.zeros((c, LANES), BF16)
    lanes0 = 0 if forward else GLA_QK
    for sub in (range(n_sub) if forward else reversed(range(n_sub))):
        r = slice(sub * c, (sub + 1) * c)
        for hd in range(GLA_HEADS):
            pair = (hd // 2) * LANES
            part = lambda i: g_ref[0, r, i * GLA_QK + pair:i * GLA_QK + pair + LANES]
            mine = lane_head == (hd % 2)
            vh = v_ref[0, r, hd * GLA_DV:(hd + 1) * GLA_DV]
            a = _dot_nt(part(0), jnp.where(mine, part(1), zero))
            a = jnp.where(tri, a, 0.0).astype(BF16)
            st = state_ref[hd]
            o = _dot(a, vh) + _dot_nt(part(2), st.astype(BF16))
            o_ref[0, r, hd * GLA_DV:(hd + 1) * GLA_DV] = o.astype(o_ref.dtype)
            decay = dec_ref[sub, :, lanes0 + pair:lanes0 + pair + LANES]
            state_ref[hd] = st * decay + _dot_tn(vh, jnp.where(mine, part(3), zero))


def _gla_kernel(gf_ref, vf_ref, df_ref, gb_ref, vb_ref, db_ref, of_ref, ob_ref, sf_ref, sb_ref, *, chunk):
    @pl.when(pl.program_id(1) == 0)
    def _():
        sf_ref[...] = jnp.zeros_like(sf_ref)
        sb_ref[...] = jnp.zeros_like(sb_ref)

    _gla_direction(gf_ref, vf_ref, df_ref, of_ref, sf_ref, True, chunk)
    _gla_direction(gb_ref, vb_ref, db_ref, ob_ref, sb_ref, False, chunk)


def _gla(gf, gb, gv, dec, chunk):
    b, s, _ = gv.shape
    per = math.gcd(s // chunk, GLA_CHUNKS_PER_STEP)
    rows = per * chunk
    n = s // rows
    fwd = lambda w: pl.BlockSpec((1, rows, w), lambda bi, ci: (bi, ci, 0))
    bwd = lambda w: pl.BlockSpec((1, rows, w), lambda bi, ci: (bi, n - 1 - ci, 0))
    dec_fwd = pl.BlockSpec((per, 1, 2 * GLA_QK), lambda bi, ci: (bi * n + ci, 0, 0))
    dec_bwd = pl.BlockSpec((per, 1, 2 * GLA_QK), lambda bi, ci: (bi * n + n - 1 - ci, 0, 0))
    out = jax.ShapeDtypeStruct((b, s, GLA_WIDTH), BF16)
    state = pltpu.VMEM((GLA_HEADS, GLA_DV, LANES), F32)
    return pl.pallas_call(
        functools.partial(_gla_kernel, chunk=chunk),
        grid=(b, n),
        in_specs=[fwd(4 * GLA_QK), fwd(GLA_WIDTH), dec_fwd, bwd(4 * GLA_QK), bwd(GLA_WIDTH), dec_bwd],
        out_specs=[fwd(GLA_WIDTH), bwd(GLA_WIDTH)],
        out_shape=[out, out],
        scratch_shapes=[state, state],
        compiler_params=pltpu.CompilerParams(
            dimension_semantics=("arbitrary", "arbitrary"), vmem_limit_bytes=VMEM_LIMIT),
        name="gla_scan",
    )(gf, gv, dec, gb, gv, dec)


def _back_kernel(h_ref, ym_ref, of_ref, ob_ref, sgg_ref, on_ref, wout_ref, pn_ref, wpg_ref, p_ref, wpp_ref,
                 fn_ref, o_ref, *, last):
    og = of_ref[...].astype(F32) + ob_ref[...].astype(F32)
    sgg = sgg_ref[...].astype(F32)
    on = on_ref[...]
    parts = [ym_ref[...]]
    for hd in range(GLA_HEADS):
        g = slice(hd * GLA_DV, (hd + 1) * GLA_DV)
        parts.append((_rms(og[:, g], on) * sgg[:, g]).astype(BF16))
    h1 = h_ref[...] + _dot(jnp.concatenate(parts, axis=1), wout_ref[...])
    gate = jax.nn.sigmoid(_dot(_rms(h1, pn_ref[...]).astype(BF16), wpg_ref[...]))
    h2 = h1 + gate * _dot(p_ref[0].astype(BF16), wpp_ref[...])
    o_ref[...] = _rms(h2, fn_ref[...]) if last else h2


def _back(h, y_mla, o_f, o_b, sgg, out_norm, w_out, ple_norm, w_pg, p, layer, w_pp, final_norm, tm, last):
    tokens, d = h.shape
    row = lambda n: pl.BlockSpec((tm, n), lambda i: (i, 0))
    full = lambda a: _layer_spec(a, layer)
    p_spec = pl.BlockSpec((1, tm, p.shape[2]), lambda i: (layer, i, 0))
    on, pn = (a.reshape(a.shape[0], 1, a.shape[1]) for a in (out_norm, ple_norm))
    fn = final_norm.reshape(1, -1)
    return pl.pallas_call(
        functools.partial(_back_kernel, last=last),
        grid=(tokens // tm,),
        in_specs=[row(d), row(MLA_WIDTH), row(GLA_WIDTH), row(GLA_WIDTH), row(GLA_WIDTH), full(on),
                  full(w_out), full(pn), full(w_pg), p_spec, full(w_pp),
                  pl.BlockSpec(fn.shape, lambda i: (0, 0))],
        out_specs=row(d),
        out_shape=jax.ShapeDtypeStruct((tokens, d), F32),
        compiler_params=pltpu.CompilerParams(dimension_semantics=("arbitrary",), vmem_limit_bytes=VMEM_LIMIT),
        name="back",
    )(h, y_mla, o_f, o_b, sgg, on, w_out, pn, w_pg, p, w_pp, fn)


def kernel(x, p, positions, ln_mix, w_in, mla_q_norm, w_uq, mla_kv_norm, w_ukv, gla_w_gate_fwd, gla_b_gate_fwd,
           gla_w_gate_bwd, gla_b_gate_bwd, gla_out_norm, w_out, ple_norm, w_ple_gate, w_ple_proj, final_norm):
    batch, seq, d = x.shape
    depth = w_in.shape[0]
    tokens = batch * seq
    t = _tiles(batch, seq)
    tabs = _rope_tables(positions, t["rope_rows"])
    h = x.reshape(tokens, d)
    p_all = p.reshape(depth, tokens, p.shape[-1])
    seq3 = lambda a: a.reshape(batch, seq, a.shape[-1])
    weights = _front_weights(w_in, w_uq, w_ukv, gla_w_gate_fwd, gla_b_gate_fwd, gla_w_gate_bwd, gla_b_gate_bwd)
    w_out, w_ple_gate, w_ple_proj = (w.astype(BF16) for w in (w_out, w_ple_gate, w_ple_proj))
    for i in range(depth):
        q, k, v, sga, gf, gb, dec, gv, sgg = _front(
            h, ln_mix, weights, mla_q_norm, mla_kv_norm, tabs, i, t["row_tile"], t["gla_chunk"])
        y_mla = _attention(seq3(q), seq3(k), seq3(v), seq3(sga), t["q_tile"])
        o_f, o_b = _gla(seq3(gf), seq3(gb), seq3(gv), dec, t["gla_chunk"])
        h = _back(h, y_mla.reshape(tokens, -1), o_f.reshape(tokens, -1), o_b.reshape(tokens, -1), sgg,
                  gla_out_norm, w_out, ple_norm, w_ple_gate, p_all, i,
                  w_ple_proj, final_norm, t["back_tile"], last=(i == depth - 1))
    return h.reshape(batch, seq, d)
```

```python
import functools
import math

import jax
import jax.numpy as jnp
from jax import lax
from jax.experimental import pallas as pl
from jax.experimental.pallas import tpu as pltpu

EPS = 1e-6
MLA_HEADS = 4
MLA_Q_RANK = 384
MLA_KV_RANK = 256
MLA_NOPE = 128
MLA_ROPE = 64
MLA_V = 128
MLA_WIDTH = MLA_HEADS * MLA_V
ROPE_THETA = 10000.0
GLA_HEADS = 4
GLA_DK = 64
GLA_DV = 128
GLA_WIDTH = GLA_HEADS * GLA_DV
GLA_QK = GLA_HEADS * GLA_DK
GLA_GATE_RANK = 16
GLA_TAU = 16.0
GLA_DEC = 2 * 3 * GLA_QK

LANES = 128
QK_PAD = 2 * LANES
VMEM_LIMIT = 48 * 1024 * 1024

Q_SCALE = (MLA_NOPE + MLA_ROPE) ** -0.5 * math.log2(math.e)

F32 = jnp.float32
BF16 = jnp.bfloat16


def _tiles(batch, seq):
    tokens = batch * seq
    row_tile = math.gcd(tokens, 512)
    return dict(
        row_tile=row_tile,
        back_tile=math.gcd(tokens, 1024),
        q_tile=math.gcd(seq, 1024),
        gla_chunk=math.gcd(seq, 256),
        rope_rows=math.gcd(tokens // 4, 1024),
    )


def _dot(a, b):
    return jnp.dot(a, b, preferred_element_type=F32)


def _dot_nt(a, b):
    return lax.dot_general(a, b, (((1,), (1,)), ((), ())), preferred_element_type=F32)


def _dot_tn(a, b):
    return lax.dot_general(a, b, (((0,), (0,)), ((), ())), preferred_element_type=F32)


def _rms(x, g):
    return x * lax.rsqrt(jnp.mean(x * x, axis=-1, keepdims=True) + EPS) * g


def _silu(x):
    return x * jax.nn.sigmoid(x)


def _log2_sigmoid(x):
    y = x * math.log2(math.e)
    return jnp.minimum(y, 0.0) - jnp.log2(1.0 + jnp.exp2(-jnp.abs(y)))


def _split2(x):
    hi = x.astype(BF16)
    lo = (x - hi.astype(F32)).astype(BF16)
    return hi, lo


def _rope_table_kernel(pos_ref, inv_ref, tq_ref, tkc_ref, tks_ref):
    half = MLA_ROPE // 2
    ang = pos_ref[...] * inv_ref[...]
    cos = jnp.cos(ang)
    sin = jnp.sin(ang)
    lane = lax.broadcasted_iota(jnp.int32, cos.shape, 1)
    for i in range(LANES // half):
        place = lambda x, dst: x if dst == i else pltpu.roll(x, ((dst - i) * half) % LANES, axis=1)
        base = jnp.where(lane < half, place(cos, 0),
                         jnp.where(lane < 2 * half, place(cos, 1),
                                   jnp.where(lane < 3 * half, -place(sin, 2), place(sin, 3))))
        tq_ref[i] = base * Q_SCALE
        tkc_ref[i] = jnp.where(lane < 2 * half, base, 0.0)
        tks_ref[i] = jnp.where(lane < 2 * half, pltpu.roll(base, 2 * half, axis=1), 0.0)


def _rope_tables(positions, rows):
    half = MLA_ROPE // 2
    per_row = LANES // half
    tokens = positions.size
    n = tokens // per_row
    inv = ROPE_THETA ** (-jnp.arange(half, dtype=F32) / half)
    pos = jnp.broadcast_to(positions.reshape(per_row, n, 1).astype(F32), (per_row, n, half))
    pos = pos.transpose(1, 0, 2).reshape(n, LANES)
    inv4 = jnp.tile(inv, per_row).reshape(1, LANES)
    out_spec = pl.BlockSpec((per_row, rows, LANES), lambda i: (0, i, 0))
    tabs = pl.pallas_call(
        _rope_table_kernel,
        grid=(n // rows,),
        in_specs=[pl.BlockSpec((rows, LANES), lambda i: (i, 0)), pl.BlockSpec((1, LANES), lambda i: (0, 0))],
        out_specs=[out_spec] * 3,
        out_shape=[jax.ShapeDtypeStruct((per_row, n, LANES), F32)] * 3,
        name="rope_tables",
    )(pos, inv4)
    return tuple(t.reshape(tokens, LANES) for t in tabs)


_C_CQ = 0
_C_KRLR = _C_CQ + MLA_Q_RANK
_C_CKV = _C_KRLR + LANES
_C_GA = _C_CKV + MLA_KV_RANK
_C_GQ = _C_GA + MLA_WIDTH
_C_GK = _C_GQ + GLA_QK
_C_GV = _C_GK + GLA_QK
_C_GG = _C_GV + GLA_WIDTH
_C_END = _C_GG + GLA_WIDTH
_LR_LANE = MLA_ROPE


def _front_kernel(h_ref, ln_ref, win_ref, qn_ref, wq_ref, kvn_ref, wkv_ref,
                  wg_ref, bg_ref, tq_ref, tkc_ref, tks_ref,
                  q_ref, k_ref, v_ref, sga_ref, gf_ref, gb_ref, dec_ref, gv_ref, sgg_ref, *, chunk):
    xn = _rms(h_ref[...], ln_ref[...]).astype(BF16)
    u = _dot(xn, win_ref[...])

    def proj(c0, c1):
        return u[:, c0:c1]

    cq_krlr = proj(_C_CQ, _C_CKV)
    krlr = cq_krlr[:, _C_KRLR:_C_CKV]

    cqn = _rms(cq_krlr[:, :_C_KRLR], qn_ref[...]).astype(BF16)
    q_all = _dot(cqn, wq_ref[...])
    q_nope = q_all[:, :MLA_HEADS * MLA_NOPE] * Q_SCALE
    q_rope = q_all[:, MLA_HEADS * MLA_NOPE:]
    tq = tq_ref[...]
    for hd in range(MLA_HEADS):
        g = slice(hd * LANES, (hd + 1) * LANES)
        q_ref[:, hd * QK_PAD:hd * QK_PAD + LANES] = q_nope[:, g].astype(BF16)
        qr = q_rope[:, g] * tq
        q_ref[:, hd * QK_PAD + LANES:(hd + 1) * QK_PAD] = (qr + pltpu.roll(qr, MLA_ROPE, axis=1)).astype(BF16)

    ckvn = _rms(proj(_C_CKV, _C_GA), kvn_ref[...]).astype(BF16)
    kv = _dot(ckvn, wkv_ref[...])
    k_nope = kv[:, :MLA_HEADS * MLA_NOPE]
    v_ref[...] = kv[:, MLA_HEADS * MLA_NOPE:].astype(BF16)
    half = MLA_ROPE // 2
    lane = lax.broadcasted_iota(jnp.int32, krlr.shape, 1)
    kr_sw = jnp.where(lane < half, pltpu.roll(krlr, LANES - half, axis=1), pltpu.roll(krlr, half, axis=1))
    kr = (krlr * tkc_ref[...] + kr_sw * tks_ref[...]).astype(BF16)
    for hd in range(MLA_HEADS):
        k_ref[:, hd * QK_PAD:hd * QK_PAD + LANES] = k_nope[:, hd * LANES:(hd + 1) * LANES].astype(BF16)
        k_ref[:, hd * QK_PAD + LANES:(hd + 1) * QK_PAD] = kr

    sga_ref[...] = _silu(proj(_C_GA, _C_GQ)).astype(BF16)
    sgg_ref[...] = _silu(proj(_C_GG, _C_END)).astype(BF16)

    gq = proj(_C_GQ, _C_GK) * GLA_DK ** -0.5
    gk = proj(_C_GK, _C_GV)
    gv_ref[...] = proj(_C_GV, _C_GG).astype(BF16)
    la = _log2_sigmoid(_dot(krlr.astype(BF16), wg_ref[...]) + bg_ref[...])
    row = lax.broadcasted_iota(jnp.int32, (chunk, chunk), 0)
    col = lax.broadcasted_iota(jnp.int32, (chunk, chunk), 1)
    for d, out_ref in enumerate((gf_ref, gb_ref)):
        forward = d == 0
        tri = jnp.where((col <= row) if forward else (col >= row), 1.0 / GLA_TAU, 0.0).astype(BF16)
        for ci in range(la.shape[0] // chunk):
            r = slice(ci * chunk, (ci + 1) * chunk)
            hi, lo = _split2(la[r, d * GLA_QK:(d + 1) * GLA_QK])
            b = _dot(tri, jnp.concatenate([hi, lo], axis=1))
            b = b[:, :GLA_QK] + b[:, GLA_QK:]
            mid = chunk // 2 - 1 if forward else chunk // 2
            end = chunk - 1 if forward else 0
            b_mid = b[mid:mid + 1, :]
            b_end = b[end:end + 1, :]
            out_ref[r, :GLA_QK] = (gq[r] * jnp.exp2(b - b_mid)).astype(BF16)
            out_ref[r, GLA_QK:] = (gk[r] * jnp.exp2(b_mid - b)).astype(BF16)
            base = d * 3 * GLA_QK
            dec_ref[ci, :, base:base + GLA_QK] = jnp.exp2(b_end)
            dec_ref[ci, :, base + GLA_QK:base + 2 * GLA_QK] = jnp.exp2(b_mid)
            dec_ref[ci, :, base + 2 * GLA_QK:base + 3 * GLA_QK] = jnp.exp2(b_end - b_mid)


def _front_weights(w_in, w_uq, w_ukv, wgf, bgf, wgb, bgb):
    depth, d, _ = w_in.shape
    c = 0
    parts = {}
    for name, size in (("cq", MLA_Q_RANK), ("ckv", MLA_KV_RANK), ("kr", MLA_ROPE), ("ga", MLA_WIDTH),
                       ("gq", GLA_QK), ("gk", GLA_QK), ("gv", GLA_WIDTH), ("lrf", GLA_GATE_RANK),
                       ("lrb", GLA_GATE_RANK), ("gg", GLA_WIDTH)):
        parts[name] = w_in[..., c:c + size]
        c += size
    half = MLA_ROPE // 2
    swap = lambda w: jnp.concatenate([w[..., half:], w[..., :half]], axis=-1)
    zeros = lambda n: jnp.zeros((depth, d, n), w_in.dtype)
    win = jnp.concatenate([
        parts["cq"],
        parts["kr"], parts["lrf"], parts["lrb"], zeros(LANES - MLA_ROPE - 2 * GLA_GATE_RANK),
        parts["ckv"], parts["ga"], parts["gq"], parts["gk"], parts["gv"], parts["gg"]], axis=-1).astype(BF16)

    wq = w_uq.reshape(depth, MLA_Q_RANK, MLA_HEADS, MLA_NOPE + MLA_ROPE)
    wqn = wq[..., :MLA_NOPE].reshape(depth, MLA_Q_RANK, MLA_HEADS * MLA_NOPE).astype(BF16)
    wr = wq[..., MLA_NOPE:]
    wqr = jnp.concatenate([wr, swap(wr)], axis=-1).reshape(depth, MLA_Q_RANK, MLA_HEADS * LANES).astype(BF16)

    wkv = w_ukv.reshape(depth, MLA_KV_RANK, MLA_HEADS, MLA_NOPE + MLA_V)
    wk = wkv[..., :MLA_NOPE].reshape(depth, MLA_KV_RANK, MLA_HEADS * MLA_NOPE).astype(BF16)
    wv = wkv[..., MLA_NOPE:].reshape(depth, MLA_KV_RANK, MLA_HEADS * MLA_V).astype(BF16)

    wg = jnp.zeros((depth, LANES, 2 * GLA_QK), F32)
    wg = wg.at[:, _LR_LANE:_LR_LANE + GLA_GATE_RANK, :GLA_QK].set(wgf)
    wg = wg.at[:, _LR_LANE + GLA_GATE_RANK:_LR_LANE + 2 * GLA_GATE_RANK, GLA_QK:].set(wgb)
    bg = jnp.concatenate([bgf, bgb], axis=-1).reshape(depth, 1, 2 * GLA_QK)
    wq = jnp.concatenate([wqn, wqr], axis=-1)
    wkv = jnp.concatenate([wk, wv], axis=-1)
    return win, wq, wkv, wg.astype(BF16), bg


def _layer_spec(a, layer):
    return pl.BlockSpec((None,) + a.shape[1:], lambda i: (layer, 0, 0))


def _front(h, ln, weights, qn, kvn, tabs, layer, tm, chunk):
    tokens, d = h.shape
    assert tm % chunk == 0, (tm, chunk)
    win, wq, wkv, wg, bg = weights
    tq, tkc, tks = tabs
    row = lambda n: pl.BlockSpec((tm, n), lambda i: (i, 0))
    full = lambda a: _layer_spec(a, layer)
    ln, qn, kvn = (a.reshape(a.shape[0], 1, a.shape[1]) for a in (ln, qn, kvn))
    outs = [
        (MLA_HEADS * QK_PAD, BF16),
        (MLA_HEADS * QK_PAD, BF16),
        (MLA_WIDTH, BF16),
        (MLA_WIDTH, BF16),
        (2 * GLA_QK, BF16),
        (2 * GLA_QK, BF16),
        None,
        (GLA_WIDTH, BF16),
        (GLA_WIDTH, BF16),
    ]
    dec_spec = pl.BlockSpec((tm // chunk, 1, GLA_DEC), lambda i: (i, 0, 0))
    dec_shape = jax.ShapeDtypeStruct((tokens // chunk, 1, GLA_DEC), F32)
    return pl.pallas_call(
        functools.partial(_front_kernel, chunk=chunk),
        grid=(tokens // tm,),
        in_specs=[row(d), full(ln), full(win), full(qn), full(wq), full(kvn),
                  full(wkv), full(wg), full(bg), row(LANES), row(LANES), row(LANES)],
        out_specs=[dec_spec if o is None else row(o[0]) for o in outs],
        out_shape=[dec_shape if o is None else jax.ShapeDtypeStruct((tokens, o[0]), o[1]) for o in outs],
        compiler_params=pltpu.CompilerParams(dimension_semantics=("arbitrary",), vmem_limit_bytes=VMEM_LIMIT),
        name="front",
    )(h, ln, win, qn, wq, kvn, wkv, wg, bg, tq, tkc, tks)


def _attn_kernel(q_ref, k_ref, v_ref, g_ref, o_ref, s_ref, macc_ref, acc_ref, vext_ref, *, tq):
    vext_ref[:, :MLA_V] = v_ref[0]
    vext_ref[:, MLA_V:] = jnp.ones((v_ref.shape[1], MLA_V), BF16)
    groups = k_ref.shape[1] // LANES
    once = jnp.minimum(pl.program_id(0) + 1, 1)

    @pl.loop(0, q_ref.shape[1] // tq)
    def _(qi):
        rows = pl.ds(pl.multiple_of(qi * tq, tq), tq)

        @pl.loop(0, once)
        def _(_):
            s = _dot_nt(q_ref[0, rows, :], k_ref[0])
            s_ref[...] = s
            m_acc = s[:, :LANES]
            for c in range(1, groups):
                m_acc = jnp.maximum(m_acc, s[:, c * LANES:(c + 1) * LANES])
            macc_ref[...] = m_acc

        m_b = jnp.broadcast_to(jnp.max(macc_ref[...], axis=-1, keepdims=True), (tq, LANES))

        @pl.loop(0, once)
        def _(_):
            cols = [jnp.exp2((s_ref[:, c * LANES:(c + 1) * LANES] - m_b).astype(BF16)) for c in range(groups)]
            acc_ref[...] = _dot(jnp.concatenate(cols, axis=1), vext_ref[...])

        acc = acc_ref[...]
        o_ref[0, rows, :] = (acc[:, :MLA_V] / acc[:, MLA_V:] * g_ref[0, rows, :].astype(F32)).astype(o_ref.dtype)


def _attention(q, k, v, sgate, tq):
    b, s, _ = q.shape
    head = lambda w: pl.BlockSpec((1, s, w), lambda bi, hi: (bi, 0, hi))
    return pl.pallas_call(
        functools.partial(_attn_kernel, tq=tq),
        grid=(b, MLA_HEADS),
        in_specs=[head(QK_PAD), head(QK_PAD), head(MLA_V), head(MLA_V)],
        out_specs=head(MLA_V),
        out_shape=jax.ShapeDtypeStruct((b, s, MLA_WIDTH), BF16),
        scratch_shapes=[pltpu.VMEM((tq, s), F32), pltpu.VMEM((tq, LANES), F32), pltpu.VMEM((tq, 2 * MLA_V), F32),
                        pltpu.VMEM((s, 2 * MLA_V), BF16)],
        compiler_params=pltpu.CompilerParams(
            dimension_semantics=("arbitrary", "arbitrary"), vmem_limit_bytes=VMEM_LIMIT),
        name="mla_attention",
    )(q, k, v, sgate)


GLA_CHUNKS_PER_STEP = 8


def _gla_direction(g_ref, v_ref, dec_ref, o_ref, state_ref, forward, chunk):
    c = chunk
    n_sub = g_ref.shape[1] // c
    row = lax.broadcasted_iota(jnp.int32, (c, c), 0)
    col = lax.broadcasted_iota(jnp.int32, (c, c), 1)
    tri = (col <= row) if forward else (col >= row)
    lane_head = lax.broadcasted_iota(jnp.int32, (c, LANES), 1) // GLA_DK
    zero = jnp.zeros((c, LANES), BF16)
    base = 0 if forward else 3 * GLA_QK
    for sub in (range(n_sub) if forward else reversed(range(n_sub))):
        r = slice(sub * c, (sub + 1) * c)
        for hd in range(GLA_HEADS):
            pair = (hd // 2) * LANES
            fac = lambda i: dec_ref[sub, :, base + i * GLA_QK + pair:base + i * GLA_QK + pair + LANES]
            q_in = g_ref[0, r, pair:pair + LANES]
            k_in = jnp.where(lane_head == (hd % 2), g_ref[0, r, GLA_QK + pair:GLA_QK + pair + LANES], zero)
            vh = v_ref[0, r, hd * GLA_DV:(hd + 1) * GLA_DV]
            a = jnp.where(tri, _dot_nt(q_in, k_in), 0.0).astype(BF16)
            st = state_ref[hd]
            o = _dot(a, vh) + _dot_nt(q_in, (st * fac(1)).astype(BF16))
            o_ref[0, r, hd * GLA_DV:(hd + 1) * GLA_DV] = o.astype(o_ref.dtype)
            state_ref[hd] = st * fac(0) + _dot_tn(vh, k_in) * fac(2)


def _gla_kernel(gf_ref, vf_ref, df_ref, gb_ref, vb_ref, db_ref, of_ref, ob_ref, sf_ref, sb_ref, *, chunk):
    @pl.when(pl.program_id(1) == 0)
    def _():
        sf_ref[...] = jnp.zeros_like(sf_ref)
        sb_ref[...] = jnp.zeros_like(sb_ref)

    _gla_direction(gf_ref, vf_ref, df_ref, of_ref, sf_ref, True, chunk)
    _gla_direction(gb_ref, vb_ref, db_ref, ob_ref, sb_ref, False, chunk)


def _gla(gf, gb, gv, dec, chunk):
    b, s, _ = gv.shape
    per = math.gcd(s // chunk, GLA_CHUNKS_PER_STEP)
    rows = per * chunk
    n = s // rows
    fwd = lambda w: pl.BlockSpec((1, rows, w), lambda bi, ci: (bi, ci, 0))
    bwd = lambda w: pl.BlockSpec((1, rows, w), lambda bi, ci: (bi, n - 1 - ci, 0))
    dec_fwd = pl.BlockSpec((per, 1, GLA_DEC), lambda bi, ci: (bi * n + ci, 0, 0))
    dec_bwd = pl.BlockSpec((per, 1, GLA_DEC), lambda bi, ci: (bi * n + n - 1 - ci, 0, 0))
    out = jax.ShapeDtypeStruct((b, s, GLA_WIDTH), BF16)
    state = pltpu.VMEM((GLA_HEADS, GLA_DV, LANES), F32)
    return pl.pallas_call(
        functools.partial(_gla_kernel, chunk=chunk),
        grid=(b, n),
        in_specs=[fwd(2 * GLA_QK), fwd(GLA_WIDTH), dec_fwd, bwd(2 * GLA_QK), bwd(GLA_WIDTH), dec_bwd],
        out_specs=[fwd(GLA_WIDTH), bwd(GLA_WIDTH)],
        out_shape=[out, out],
        scratch_shapes=[state, state],
        compiler_params=pltpu.CompilerParams(
            dimension_semantics=("arbitrary", "arbitrary"), vmem_limit_bytes=VMEM_LIMIT),
        name="gla_scan",
    )(gf, gv, dec, gb, gv, dec)


def _back_kernel(h_ref, ym_ref, of_ref, ob_ref, sgg_ref, on_ref, wout_ref, pn_ref, wpg_ref, p_ref, wpp_ref,
                 fn_ref, o_ref, *, last):
    og = of_ref[...].astype(F32) + ob_ref[...].astype(F32)
    sgg = sgg_ref[...].astype(F32)
    on = on_ref[...]
    parts = [ym_ref[...]]
    for hd in range(GLA_HEADS):
        g = slice(hd * GLA_DV, (hd + 1) * GLA_DV)
        parts.append((_rms(og[:, g], on) * sgg[:, g]).astype(BF16))
    h1 = h_ref[...] + _dot(jnp.concatenate(parts, axis=1), wout_ref[...])
    gate = jax.nn.sigmoid(_dot(_rms(h1, pn_ref[...]).astype(BF16), wpg_ref[...]))
    h2 = h1 + gate * _dot(p_ref[0].astype(BF16), wpp_ref[...])
    o_ref[...] = _rms(h2, fn_ref[...]) if last else h2


def _back(h, y_mla, o_f, o_b, sgg, out_norm, w_out, ple_norm, w_pg, p, layer, w_pp, final_norm, tm, last):
    tokens, d = h.shape
    row = lambda n: pl.BlockSpec((tm, n), lambda i: (i, 0))
    full = lambda a: _layer_spec(a, layer)
    p_spec = pl.BlockSpec((1, tm, p.shape[2]), lambda i: (layer, i, 0))
    on, pn = (a.reshape(a.shape[0], 1, a.shape[1]) for a in (out_norm, ple_norm))
    fn = final_norm.reshape(1, -1)
    return pl.pallas_call(
        functools.partial(_back_kernel, last=last),
        grid=(tokens // tm,),
        in_specs=[row(d), row(MLA_WIDTH), row(GLA_WIDTH), row(GLA_WIDTH), row(GLA_WIDTH), full(on),
                  full(w_out), full(pn), full(w_pg), p_spec, full(w_pp),
                  pl.BlockSpec(fn.shape, lambda i: (0, 0))],
        out_specs=row(d),
        out_shape=jax.ShapeDtypeStruct((tokens, d), F32),
        compiler_params=pltpu.CompilerParams(dimension_semantics=("arbitrary",), vmem_limit_bytes=VMEM_LIMIT),
        name="back",
    )(h, y_mla, o_f, o_b, sgg, on, w_out, pn, w_pg, p, w_pp, fn)


def kernel(x, p, positions, ln_mix, w_in, mla_q_norm, w_uq, mla_kv_norm, w_ukv, gla_w_gate_fwd, gla_b_gate_fwd,
           gla_w_gate_bwd, gla_b_gate_bwd, gla_out_norm, w_out, ple_norm, w_ple_gate, w_ple_proj, final_norm):
    batch, seq, d = x.shape
    depth = w_in.shape[0]
    tokens = batch * seq
    t = _tiles(batch, seq)
    tabs = _rope_tables(positions, t["rope_rows"])
    h = x.reshape(tokens, d)
    p_all = p.reshape(depth, tokens, p.shape[-1])
    seq3 = lambda a: a.reshape(batch, seq, a.shape[-1])
    weights = _front_weights(w_in, w_uq, w_ukv, gla_w_gate_fwd, gla_b_gate_fwd, gla_w_gate_bwd, gla_b_gate_bwd)
    w_out, w_ple_gate, w_ple_proj = (w.astype(BF16) for w in (w_out, w_ple_gate, w_ple_proj))
    for i in range(depth):
        q, k, v, sga, gf, gb, dec, gv, sgg = _front(
            h, ln_mix, weights, mla_q_norm, mla_kv_norm, tabs, i, t["row_tile"], t["gla_chunk"])
        y_mla = _attention(seq3(q), seq3(k), seq3(v), seq3(sga), t["q_tile"])
        o_f, o_b = _gla(seq3(gf), seq3(gb), seq3(gv), dec, t["gla_chunk"])
        h = _back(h, y_mla.reshape(tokens, -1), o_f.reshape(tokens, -1), o_b.reshape(tokens, -1), sgg,
                  gla_out_norm, w_out, ple_norm, w_ple_gate, p_all, i,
                  w_ple_proj, final_norm, t["back_tile"], last=(i == depth - 1))
    return h.reshape(batch, seq, d)
```

```python
import functools
import math

import jax
import jax.numpy as jnp
from jax import lax
from jax.experimental import pallas as pl
from jax.experimental.pallas import tpu as pltpu

EPS = 1e-6
MLA_HEADS = 4
MLA_Q_RANK = 384
MLA_KV_RANK = 256
MLA_NOPE = 128
MLA_ROPE = 64
MLA_V = 128
MLA_WIDTH = MLA_HEADS * MLA_V
ROPE_THETA = 10000.0
GLA_HEADS = 4
GLA_DK = 64
GLA_DV = 128
GLA_WIDTH = GLA_HEADS * GLA_DV
GLA_QK = GLA_HEADS * GLA_DK
GLA_GATE_RANK = 16
GLA_TAU = 16.0
GLA_DEC = 2 * 3 * GLA_QK

LANES = 128
QK_PAD = 2 * LANES
VMEM_LIMIT = 48 * 1024 * 1024

Q_SCALE = (MLA_NOPE + MLA_ROPE) ** -0.5 * math.log2(math.e)

F32 = jnp.float32
BF16 = jnp.bfloat16


def _tiles(batch, seq):
    tokens = batch * seq
    row_tile = math.gcd(tokens, 512)
    return dict(
        row_tile=row_tile,
        back_tile=math.gcd(tokens, 1024),
        q_tile=math.gcd(seq, 1024),
        gla_chunk=math.gcd(seq, 256),
        rope_rows=math.gcd(tokens // 4, 1024),
    )


def _dot(a, b):
    return jnp.dot(a, b, preferred_element_type=F32)


def _dot_nt(a, b):
    return lax.dot_general(a, b, (((1,), (1,)), ((), ())), preferred_element_type=F32)


def _dot_tn(a, b):
    return lax.dot_general(a, b, (((0,), (0,)), ((), ())), preferred_element_type=F32)


def _rms(x, g):
    return x * lax.rsqrt(jnp.mean(x * x, axis=-1, keepdims=True) + EPS) * g


def _silu(x):
    return x * jax.nn.sigmoid(x)


def _log2_sigmoid(x):
    y = x * math.log2(math.e)
    return jnp.minimum(y, 0.0) - jnp.log2(1.0 + jnp.exp2(-jnp.abs(y)))


def _split2(x):
    hi = x.astype(BF16)
    lo = (x - hi.astype(F32)).astype(BF16)
    return hi, lo


def _rope_table_kernel(pos_ref, inv_ref, tq_ref, tkc_ref, tks_ref):
    half = MLA_ROPE // 2
    ang = pos_ref[...] * inv_ref[...]
    cos = jnp.cos(ang)
    sin = jnp.sin(ang)
    lane = lax.broadcasted_iota(jnp.int32, cos.shape, 1)
    for i in range(LANES // half):
        place = lambda x, dst: x if dst == i else pltpu.roll(x, ((dst - i) * half) % LANES, axis=1)
        base = jnp.where(lane < half, place(cos, 0),
                         jnp.where(lane < 2 * half, place(cos, 1),
                                   jnp.where(lane < 3 * half, -place(sin, 2), place(sin, 3))))
        tq_ref[i] = base * Q_SCALE
        tkc_ref[i] = jnp.where(lane < 2 * half, base, 0.0)
        tks_ref[i] = jnp.where(lane < 2 * half, pltpu.roll(base, 2 * half, axis=1), 0.0)


def _rope_tables(positions, rows):
    half = MLA_ROPE // 2
    per_row = LANES // half
    tokens = positions.size
    n = tokens // per_row
    inv = ROPE_THETA ** (-jnp.arange(half, dtype=F32) / half)
    pos = jnp.broadcast_to(positions.reshape(per_row, n, 1).astype(F32), (per_row, n, half))
    pos = pos.transpose(1, 0, 2).reshape(n, LANES)
    inv4 = jnp.tile(inv, per_row).reshape(1, LANES)
    out_spec = pl.BlockSpec((per_row, rows, LANES), lambda i: (0, i, 0))
    tabs = pl.pallas_call(
        _rope_table_kernel,
        grid=(n // rows,),
        in_specs=[pl.BlockSpec((rows, LANES), lambda i: (i, 0)), pl.BlockSpec((1, LANES), lambda i: (0, 0))],
        out_specs=[out_spec] * 3,
        out_shape=[jax.ShapeDtypeStruct((per_row, n, LANES), F32)] * 3,
        name="rope_tables",
    )(pos, inv4)
    return tuple(t.reshape(tokens, LANES) for t in tabs)


_C_CQ = 0
_C_KRLR = _C_CQ + MLA_Q_RANK
_C_CKV = _C_KRLR + LANES
_C_GA = _C_CKV + MLA_KV_RANK
_C_GQ = _C_GA + MLA_WIDTH
_C_GK = _C_GQ + GLA_QK
_C_GV = _C_GK + GLA_QK
_C_GG = _C_GV + GLA_WIDTH
_C_END = _C_GG + GLA_WIDTH
_LR_LANE = MLA_ROPE
_W_CQ = 0
_W_CKV = _W_CQ + MLA_Q_RANK
_W_KR = _W_CKV + MLA_KV_RANK
_W_GA = _W_KR + MLA_ROPE
_W_LR = _W_GA + MLA_WIDTH + 2 * GLA_QK + GLA_WIDTH
_W_GG = _W_LR + 2 * GLA_GATE_RANK
_W_END = _W_GG + GLA_WIDTH


def _win_kernel(w_ref, o_ref):
    cast = lambda a, b: w_ref[:, a:b].astype(BF16)
    o_ref[:, _C_CQ:_C_KRLR] = cast(_W_CQ, _W_CKV)
    o_ref[:, _C_KRLR:_C_KRLR + MLA_ROPE] = cast(_W_KR, _W_GA)
    o_ref[:, _C_KRLR + MLA_ROPE:_C_KRLR + MLA_ROPE + 2 * GLA_GATE_RANK] = cast(_W_LR, _W_GG)
    o_ref[:, _C_KRLR + MLA_ROPE + 2 * GLA_GATE_RANK:_C_CKV] = jnp.zeros(
        (o_ref.shape[0], LANES - MLA_ROPE - 2 * GLA_GATE_RANK), BF16)
    o_ref[:, _C_CKV:_C_GA] = cast(_W_CKV, _W_KR)
    o_ref[:, _C_GA:_C_GG] = cast(_W_GA, _W_LR)
    o_ref[:, _C_GG:_C_END] = cast(_W_GG, _W_END)


def _permute_win(w_in, rows):
    depth, d, n = w_in.shape
    return pl.pallas_call(
        _win_kernel,
        grid=(depth, d // rows),
        in_specs=[pl.BlockSpec((None, rows, n), lambda l, i: (l, i, 0))],
        out_specs=pl.BlockSpec((None, rows, _C_END), lambda l, i: (l, i, 0)),
        out_shape=jax.ShapeDtypeStruct((depth, d, _C_END), BF16),
        name="permute_w_in",
    )(w_in)


def _front_kernel(h_ref, ln_ref, win_ref, qn_ref, wq_ref, kvn_ref, wkv_ref,
                  wg_ref, bg_ref, tq_ref, tkc_ref, tks_ref,
                  q_ref, k_ref, v_ref, sga_ref, gf_ref, gb_ref, dec_ref, gv_ref, sgg_ref, *, chunk):
    xn = _rms(h_ref[...], ln_ref[...]).astype(BF16)
    u = _dot(xn, win_ref[...])

    def proj(c0, c1):
        return u[:, c0:c1]

    cq_krlr = proj(_C_CQ, _C_CKV)
    krlr = cq_krlr[:, _C_KRLR:_C_CKV]

    cqn = _rms(cq_krlr[:, :_C_KRLR], qn_ref[...]).astype(BF16)
    q_all = _dot(cqn, wq_ref[...])
    q_nope = q_all[:, :MLA_HEADS * MLA_NOPE] * Q_SCALE
    q_rope = q_all[:, MLA_HEADS * MLA_NOPE:]
    tq = tq_ref[...]
    for hd in range(MLA_HEADS):
        g = slice(hd * LANES, (hd + 1) * LANES)
        q_ref[:, hd * QK_PAD:hd * QK_PAD + LANES] = q_nope[:, g].astype(BF16)
        qr = q_rope[:, g] * tq
        q_ref[:, hd * QK_PAD + LANES:(hd + 1) * QK_PAD] = (qr + pltpu.roll(qr, MLA_ROPE, axis=1)).astype(BF16)

    ckvn = _rms(proj(_C_CKV, _C_GA), kvn_ref[...]).astype(BF16)
    kv = _dot(ckvn, wkv_ref[...])
    k_nope = kv[:, :MLA_HEADS * MLA_NOPE]
    v_ref[...] = kv[:, MLA_HEADS * MLA_NOPE:].astype(BF16)
    half = MLA_ROPE // 2
    lane = lax.broadcasted_iota(jnp.int32, krlr.shape, 1)
    kr_sw = jnp.where(lane < half, pltpu.roll(krlr, LANES - half, axis=1), pltpu.roll(krlr, half, axis=1))
    kr = (krlr * tkc_ref[...] + kr_sw * tks_ref[...]).astype(BF16)
    for hd in range(MLA_HEADS):
        k_ref[:, hd * QK_PAD:hd * QK_PAD + LANES] = k_nope[:, hd * LANES:(hd + 1) * LANES].astype(BF16)
        k_ref[:, hd * QK_PAD + LANES:(hd + 1) * QK_PAD] = kr

    sga_ref[...] = _silu(proj(_C_GA, _C_GQ)).astype(BF16)
    sgg_ref[...] = _silu(proj(_C_GG, _C_END)).astype(BF16)

    gq = proj(_C_GQ, _C_GK) * GLA_DK ** -0.5
    gk = proj(_C_GK, _C_GV)
    gv_ref[...] = proj(_C_GV, _C_GG).astype(BF16)
    la = _log2_sigmoid(_dot(krlr.astype(BF16), wg_ref[...]) + bg_ref[...])
    row = lax.broadcasted_iota(jnp.int32, (chunk, chunk), 0)
    col = lax.broadcasted_iota(jnp.int32, (chunk, chunk), 1)
    for d, out_ref in enumerate((gf_ref, gb_ref)):
        forward = d == 0
        tri = jnp.where((col <= row) if forward else (col >= row), 1.0 / GLA_TAU, 0.0).astype(BF16)
        for ci in range(la.shape[0] // chunk):
            r = slice(ci * chunk, (ci + 1) * chunk)
            hi, lo = _split2(la[r, d * GLA_QK:(d + 1) * GLA_QK])
            b = _dot(tri, jnp.concatenate([hi, lo], axis=1))
            b = b[:, :GLA_QK] + b[:, GLA_QK:]
            mid = chunk // 2 - 1 if forward else chunk // 2
            end = chunk - 1 if forward else 0
            b_mid = b[mid:mid + 1, :]
            b_end = b[end:end + 1, :]
            out_ref[r, :GLA_QK] = (gq[r] * jnp.exp2(b - b_mid)).astype(BF16)
            out_ref[r, GLA_QK:] = (gk[r] * jnp.exp2(b_mid - b)).astype(BF16)
            base = d * 3 * GLA_QK
            dec_ref[ci, :, base:base + GLA_QK] = jnp.exp2(b_end)
            dec_ref[ci, :, base + GLA_QK:base + 2 * GLA_QK] = jnp.exp2(b_mid)
            dec_ref[ci, :, base + 2 * GLA_QK:base + 3 * GLA_QK] = jnp.exp2(b_end - b_mid)


def _front_weights(w_in, w_uq, w_ukv, wgf, bgf, wgb, bgb):
    depth = w_in.shape[0]
    half = MLA_ROPE // 2
    swap = lambda w: jnp.concatenate([w[..., half:], w[..., :half]], axis=-1)
    win = _permute_win(w_in, math.gcd(w_in.shape[1], 256))

    wq = w_uq.reshape(depth, MLA_Q_RANK, MLA_HEADS, MLA_NOPE + MLA_ROPE)
    wqn = wq[..., :MLA_NOPE].reshape(depth, MLA_Q_RANK, MLA_HEADS * MLA_NOPE).astype(BF16)
    wr = wq[..., MLA_NOPE:]
    wqr = jnp.concatenate([wr, swap(wr)], axis=-1).reshape(depth, MLA_Q_RANK, MLA_HEADS * LANES).astype(BF16)

    wkv = w_ukv.reshape(depth, MLA_KV_RANK, MLA_HEADS, MLA_NOPE + MLA_V)
    wk = wkv[..., :MLA_NOPE].reshape(depth, MLA_KV_RANK, MLA_HEADS * MLA_NOPE).astype(BF16)
    wv = wkv[..., MLA_NOPE:].reshape(depth, MLA_KV_RANK, MLA_HEADS * MLA_V).astype(BF16)

    wg = jnp.zeros((depth, LANES, 2 * GLA_QK), F32)
    wg = wg.at[:, _LR_LANE:_LR_LANE + GLA_GATE_RANK, :GLA_QK].set(wgf)
    wg = wg.at[:, _LR_LANE + GLA_GATE_RANK:_LR_LANE + 2 * GLA_GATE_RANK, GLA_QK:].set(wgb)
    bg = jnp.concatenate([bgf, bgb], axis=-1).reshape(depth, 1, 2 * GLA_QK)
    wq = jnp.concatenate([wqn, wqr], axis=-1)
    wkv = jnp.concatenate([wk, wv], axis=-1)
    return win, wq, wkv, wg.astype(BF16), bg


def _layer_spec(a, layer):
    return pl.BlockSpec((None,) + a.shape[1:], lambda i: (layer, 0, 0))


def _front(h, ln, weights, qn, kvn, tabs, layer, tm, chunk):
    tokens, d = h.shape
    assert tm % chunk == 0, (tm, chunk)
    win, wq, wkv, wg, bg = weights
    tq, tkc, tks = tabs
    row = lambda n: pl.BlockSpec((tm, n), lambda i: (i, 0))
    full = lambda a: _layer_spec(a, layer)
    ln, qn, kvn = (a.reshape(a.shape[0], 1, a.shape[1]) for a in (ln, qn, kvn))
    outs = [
        (MLA_HEADS * QK_PAD, BF16),
        (MLA_HEADS * QK_PAD, BF16),
        (MLA_WIDTH, BF16),
        (MLA_WIDTH, BF16),
        (2 * GLA_QK, BF16),
        (2 * GLA_QK, BF16),
        None,
        (GLA_WIDTH, BF16),
        (GLA_WIDTH, BF16),
    ]
    dec_spec = pl.BlockSpec((tm // chunk, 1, GLA_DEC), lambda i: (i, 0, 0))
    dec_shape = jax.ShapeDtypeStruct((tokens // chunk, 1, GLA_DEC), F32)
    return pl.pallas_call(
        functools.partial(_front_kernel, chunk=chunk),
        grid=(tokens // tm,),
        in_specs=[row(d), full(ln), full(win), full(qn), full(wq), full(kvn),
                  full(wkv), full(wg), full(bg), row(LANES), row(LANES), row(LANES)],
        out_specs=[dec_spec if o is None else row(o[0]) for o in outs],
        out_shape=[dec_shape if o is None else jax.ShapeDtypeStruct((tokens, o[0]), o[1]) for o in outs],
        compiler_params=pltpu.CompilerParams(dimension_semantics=("arbitrary",), vmem_limit_bytes=VMEM_LIMIT),
        name="front",
    )(h, ln, win, qn, wq, kvn, wkv, wg, bg, tq, tkc, tks)


def _attn_kernel(q_ref, k_ref, v_ref, g_ref, o_ref, s_ref, macc_ref, acc_ref, vext_ref, *, tq):
    vext_ref[:, :MLA_V] = v_ref[0]
    vext_ref[:, MLA_V:] = jnp.ones((v_ref.shape[1], MLA_V), BF16)
    groups = k_ref.shape[1] // LANES
    once = jnp.minimum(pl.program_id(0) + 1, 1)

    @pl.loop(0, q_ref.shape[1] // tq)
    def _(qi):
        rows = pl.ds(pl.multiple_of(qi * tq, tq), tq)

        @pl.loop(0, once)
        def _(_):
            s = _dot_nt(q_ref[0, rows, :], k_ref[0])
            s_ref[...] = s
            m_acc = s[:, :LANES]
            for c in range(1, groups):
                m_acc = jnp.maximum(m_acc, s[:, c * LANES:(c + 1) * LANES])
            macc_ref[...] = m_acc

        m_b = jnp.broadcast_to(jnp.max(macc_ref[...], axis=-1, keepdims=True), (tq, LANES))

        @pl.loop(0, once)
        def _(_):
            cols = [jnp.exp2((s_ref[:, c * LANES:(c + 1) * LANES] - m_b).astype(BF16)) for c in range(groups)]
            acc_ref[...] = _dot(jnp.concatenate(cols, axis=1), vext_ref[...])

        acc = acc_ref[...]
        o_ref[0, rows, :] = (acc[:, :MLA_V] / acc[:, MLA_V:] * g_ref[0, rows, :].astype(F32)).astype(o_ref.dtype)


def _attention(q, k, v, sgate, tq):
    b, s, _ = q.shape
    head = lambda w: pl.BlockSpec((1, s, w), lambda bi, hi: (bi, 0, hi))
    return pl.pallas_call(
        functools.partial(_attn_kernel, tq=tq),
        grid=(b, MLA_HEADS),
        in_specs=[head(QK_PAD), head(QK_PAD), head(MLA_V), head(MLA_V)],
        out_specs=head(MLA_V),
        out_shape=jax.ShapeDtypeStruct((b, s, MLA_WIDTH), BF16),
        scratch_shapes=[pltpu.VMEM((tq, s), F32), pltpu.VMEM((tq, LANES), F32), pltpu.VMEM((tq, 2 * MLA_V), F32),
                        pltpu.VMEM((s, 2 * MLA_V), BF16)],
        compiler_params=pltpu.CompilerParams(
            dimension_semantics=("arbitrary", "arbitrary"), vmem_limit_bytes=VMEM_LIMIT),
        name="mla_attention",
    )(q, k, v, sgate)


GLA_CHUNKS_PER_STEP = 8


def _gla_direction(g_ref, v_ref, dec_ref, o_ref, state_ref, forward, chunk):
    c = chunk
    n_sub = g_ref.shape[1] // c
    row = lax.broadcasted_iota(jnp.int32, (c, c), 0)
    col = lax.broadcasted_iota(jnp.int32, (c, c), 1)
    tri = (col <= row) if forward else (col >= row)
    lane_head = lax.broadcasted_iota(jnp.int32, (c, LANES), 1) // GLA_DK
    zero = jnp.zeros((c, LANES), BF16)
    base = 0 if forward else 3 * GLA_QK
    for sub in (range(n_sub) if forward else reversed(range(n_sub))):
        r = slice(sub * c, (sub + 1) * c)
        for hd in range(GLA_HEADS):
            pair = (hd // 2) * LANES
            fac = lambda i: dec_ref[sub, :, base + i * GLA_QK + pair:base + i * GLA_QK + pair + LANES]
            q_in = g_ref[0, r, pair:pair + LANES]
            k_in = jnp.where(lane_head == (hd % 2), g_ref[0, r, GLA_QK + pair:GLA_QK + pair + LANES], zero)
            vh = v_ref[0, r, hd * GLA_DV:(hd + 1) * GLA_DV]
            a = jnp.where(tri, _dot_nt(q_in, k_in), 0.0).astype(BF16)
            st = state_ref[hd]
            o = _dot(a, vh) + _dot_nt(q_in, (st * fac(1)).astype(BF16))
            o_ref[0, r, hd * GLA_DV:(hd + 1) * GLA_DV] = o.astype(o_ref.dtype)
            state_ref[hd] = st * fac(0) + _dot_tn(vh, k_in) * fac(2)


def _gla_kernel(gf_ref, vf_ref, df_ref, gb_ref, vb_ref, db_ref, of_ref, ob_ref, sf_ref, sb_ref, *, chunk):
    @pl.when(pl.program_id(1) == 0)
    def _():
        sf_ref[...] = jnp.zeros_like(sf_ref)
        sb_ref[...] = jnp.zeros_like(sb_ref)

    _gla_direction(gf_ref, vf_ref, df_ref, of_ref, sf_ref, True, chunk)
    _gla_direction(gb_ref, vb_ref, db_ref, ob_ref, sb_ref, False, chunk)


def _gla(gf, gb, gv, dec, chunk):
    b, s, _ = gv.shape
    per = math.gcd(s // chunk, GLA_CHUNKS_PER_STEP)
    rows = per * chunk
    n = s // rows
    fwd = lambda w: pl.BlockSpec((1, rows, w), lambda bi, ci: (bi, ci, 0))
    bwd = lambda w: pl.BlockSpec((1, rows, w), lambda bi, ci: (bi, n - 1 - ci, 0))
    dec_fwd = pl.BlockSpec((per, 1, GLA_DEC), lambda bi, ci: (bi * n + ci, 0, 0))
    dec_bwd = pl.BlockSpec((per, 1, GLA_DEC), lambda bi, ci: (bi * n + n - 1 - ci, 0, 0))
    out = jax.ShapeDtypeStruct((b, s, GLA_WIDTH), BF16)
    state = pltpu.VMEM((GLA_HEADS, GLA_DV, LANES), F32)
    return pl.pallas_call(
        functools.partial(_gla_kernel, chunk=chunk),
        grid=(b, n),
        in_specs=[fwd(2 * GLA_QK), fwd(GLA_WIDTH), dec_fwd, bwd(2 * GLA_QK), bwd(GLA_WIDTH), dec_bwd],
        out_specs=[fwd(GLA_WIDTH), bwd(GLA_WIDTH)],
        out_shape=[out, out],
        scratch_shapes=[state, state],
        compiler_params=pltpu.CompilerParams(
            dimension_semantics=("arbitrary", "arbitrary"), vmem_limit_bytes=VMEM_LIMIT),
        name="gla_scan",
    )(gf, gv, dec, gb, gv, dec)


def _back_kernel(h_ref, ym_ref, of_ref, ob_ref, sgg_ref, on_ref, wout_ref, pn_ref, wpg_ref, p_ref, wpp_ref,
                 fn_ref, o_ref, *, last):
    og = of_ref[...].astype(F32) + ob_ref[...].astype(F32)
    sgg = sgg_ref[...].astype(F32)
    on = on_ref[...]
    parts = [ym_ref[...]]
    for hd in range(GLA_HEADS):
        g = slice(hd * GLA_DV, (hd + 1) * GLA_DV)
        parts.append((_rms(og[:, g], on) * sgg[:, g]).astype(BF16))
    h1 = h_ref[...] + _dot(jnp.concatenate(parts, axis=1), wout_ref[...])
    gate = jax.nn.sigmoid(_dot(_rms(h1, pn_ref[...]).astype(BF16), wpg_ref[...]))
    h2 = h1 + gate * _dot(p_ref[0].astype(BF16), wpp_ref[...])
    o_ref[...] = _rms(h2, fn_ref[...]) if last else h2


def _back(h, y_mla, o_f, o_b, sgg, out_norm, w_out, ple_norm, w_pg, p, layer, w_pp, final_norm, tm, last):
    tokens, d = h.shape
    row = lambda n: pl.BlockSpec((tm, n), lambda i: (i, 0))
    full = lambda a: _layer_spec(a, layer)
    p_spec = pl.BlockSpec((1, tm, p.shape[2]), lambda i: (layer, i, 0))
    on, pn = (a.reshape(a.shape[0], 1, a.shape[1]) for a in (out_norm, ple_norm))
    fn = final_norm.reshape(1, -1)
    return pl.pallas_call(
        functools.partial(_back_kernel, last=last),
        grid=(tokens // tm,),
        in_specs=[row(d), row(MLA_WIDTH), row(GLA_WIDTH), row(GLA_WIDTH), row(GLA_WIDTH), full(on),
                  full(w_out), full(pn), full(w_pg), p_spec, full(w_pp),
                  pl.BlockSpec(fn.shape, lambda i: (0, 0))],
        out_specs=row(d),
        out_shape=jax.ShapeDtypeStruct((tokens, d), F32),
        compiler_params=pltpu.CompilerParams(dimension_semantics=("arbitrary",), vmem_limit_bytes=VMEM_LIMIT),
        name="back",
    )(h, y_mla, o_f, o_b, sgg, on, w_out, pn, w_pg, p, w_pp, fn)


def kernel(x, p, positions, ln_mix, w_in, mla_q_norm, w_uq, mla_kv_norm, w_ukv, gla_w_gate_fwd, gla_b_gate_fwd,
           gla_w_gate_bwd, gla_b_gate_bwd, gla_out_norm, w_out, ple_norm, w_ple_gate, w_ple_proj, final_norm):
    batch, seq, d = x.shape
    depth = w_in.shape[0]
    tokens = batch * seq
    t = _tiles(batch, seq)
    tabs = _rope_tables(positions, t["rope_rows"])
    h = x.reshape(tokens, d)
    p_all = p.reshape(depth, tokens, p.shape[-1])
    seq3 = lambda a: a.reshape(batch, seq, a.shape[-1])
    weights = _front_weights(w_in, w_uq, w_ukv, gla_w_gate_fwd, gla_b_gate_fwd, gla_w_gate_bwd, gla_b_gate_bwd)
    w_out, w_ple_gate, w_ple_proj = (w.astype(BF16) for w in (w_out, w_ple_gate, w_ple_proj))
    for i in range(depth):
        q, k, v, sga, gf, gb, dec, gv, sgg = _front(
            h, ln_mix, weights, mla_q_norm, mla_kv_norm, tabs, i, t["row_tile"], t["gla_chunk"])
        y_mla = _attention(seq3(q), seq3(k), seq3(v), seq3(sga), t["q_tile"])
        o_f, o_b = _gla(seq3(gf), seq3(gb), seq3(gv), dec, t["gla_chunk"])
        h = _back(h, y_mla.reshape(tokens, -1), o_f.reshape(tokens, -1), o_b.reshape(tokens, -1), sgg,
                  gla_out_norm, w_out, ple_norm, w_ple_gate, p_all, i,
                  w_ple_proj, final_norm, t["back_tile"], last=(i == depth - 1))
    return h.reshape(batch, seq, d)
```

```python
import functools
import math

import jax
import jax.numpy as jnp
from jax import lax
from jax.experimental import pallas as pl
from jax.experimental.pallas import tpu as pltpu

EPS = 1e-6
MLA_HEADS = 4
MLA_Q_RANK = 384
MLA_KV_RANK = 256
MLA_NOPE = 128
MLA_ROPE = 64
MLA_V = 128
MLA_WIDTH = MLA_HEADS * MLA_V
ROPE_THETA = 10000.0
GLA_HEADS = 4
GLA_DK = 64
GLA_DV = 128
GLA_WIDTH = GLA_HEADS * GLA_DV
GLA_QK = GLA_HEADS * GLA_DK
GLA_GATE_RANK = 16
GLA_TAU = 16.0
GLA_DEC = 2 * 3 * GLA_QK

LANES = 128
QK_PAD = 2 * LANES
VMEM_LIMIT = 48 * 1024 * 1024

Q_SCALE = (MLA_NOPE + MLA_ROPE) ** -0.5 * math.log2(math.e)

F32 = jnp.float32
BF16 = jnp.bfloat16


def _tiles(batch, seq):
    tokens = batch * seq
    row_tile = math.gcd(tokens, 512)
    return dict(
        row_tile=row_tile,
        back_tile=math.gcd(tokens, 1024),
        q_tile=math.gcd(seq, 1024),
        gla_chunk=math.gcd(seq, 256),
        rope_rows=math.gcd(tokens // 4, 1024),
    )


def _dot(a, b):
    return jnp.dot(a, b, preferred_element_type=F32)


def _dot_nt(a, b):
    return lax.dot_general(a, b, (((1,), (1,)), ((), ())), preferred_element_type=F32)


def _dot_tn(a, b):
    return lax.dot_general(a, b, (((0,), (0,)), ((), ())), preferred_element_type=F32)


def _rms(x, g):
    return x * lax.rsqrt(jnp.mean(x * x, axis=-1, keepdims=True) + EPS) * g


def _silu(x):
    return x * jax.nn.sigmoid(x)


def _log2_sigmoid(x):
    y = x * math.log2(math.e)
    return jnp.minimum(y, 0.0) - jnp.log2(1.0 + jnp.exp2(-jnp.abs(y)))


def _split2(x):
    hi = x.astype(BF16)
    lo = (x - hi.astype(F32)).astype(BF16)
    return hi, lo


def _rope_table_kernel(pos_ref, inv_ref, tq_ref, tkc_ref, tks_ref):
    half = MLA_ROPE // 2
    ang = pos_ref[...] * inv_ref[...]
    cos = jnp.cos(ang)
    sin = jnp.sin(ang)
    lane = lax.broadcasted_iota(jnp.int32, cos.shape, 1)
    for i in range(LANES // half):
        place = lambda x, dst: x if dst == i else pltpu.roll(x, ((dst - i) * half) % LANES, axis=1)
        base = jnp.where(lane < half, place(cos, 0),
                         jnp.where(lane < 2 * half, place(cos, 1),
                                   jnp.where(lane < 3 * half, -place(sin, 2), place(sin, 3))))
        tq_ref[i] = base * Q_SCALE
        tkc_ref[i] = jnp.where(lane < 2 * half, base, 0.0)
        tks_ref[i] = jnp.where(lane < 2 * half, pltpu.roll(base, 2 * half, axis=1), 0.0)


def _rope_tables(positions, rows):
    half = MLA_ROPE // 2
    per_row = LANES // half
    tokens = positions.size
    n = tokens // per_row
    inv = ROPE_THETA ** (-jnp.arange(half, dtype=F32) / half)
    pos = jnp.broadcast_to(positions.reshape(per_row, n, 1).astype(F32), (per_row, n, half))
    pos = pos.transpose(1, 0, 2).reshape(n, LANES)
    inv4 = jnp.tile(inv, per_row).reshape(1, LANES)
    out_spec = pl.BlockSpec((per_row, rows, LANES), lambda i: (0, i, 0))
    tabs = pl.pallas_call(
        _rope_table_kernel,
        grid=(n // rows,),
        in_specs=[pl.BlockSpec((rows, LANES), lambda i: (i, 0)), pl.BlockSpec((1, LANES), lambda i: (0, 0))],
        out_specs=[out_spec] * 3,
        out_shape=[jax.ShapeDtypeStruct((per_row, n, LANES), F32)] * 3,
        name="rope_tables",
    )(pos, inv4)
    return tuple(t.reshape(tokens, LANES) for t in tabs)


_C_CQ = 0
_C_KRLR = _C_CQ + MLA_Q_RANK
_C_CKV = _C_KRLR + LANES
_C_GA = _C_CKV + MLA_KV_RANK
_C_GQ = _C_GA + MLA_WIDTH
_C_GK = _C_GQ + GLA_QK
_C_GV = _C_GK + GLA_QK
_C_GG = _C_GV + GLA_WIDTH
_C_END = _C_GG + GLA_WIDTH
_LR_LANE = MLA_ROPE
_W_CQ = 0
_W_CKV = _W_CQ + MLA_Q_RANK
_W_KR = _W_CKV + MLA_KV_RANK
_W_GA = _W_KR + MLA_ROPE
_W_LR = _W_GA + MLA_WIDTH + 2 * GLA_QK + GLA_WIDTH
_W_GG = _W_LR + 2 * GLA_GATE_RANK
_W_END = _W_GG + GLA_WIDTH


def _win_kernel(wt_ref, o_ref):
    take = lambda a, b: wt_ref[a:b, :]
    pad = jnp.zeros((LANES - MLA_ROPE - 2 * GLA_GATE_RANK, wt_ref.shape[1]), wt_ref.dtype)
    wt = jnp.concatenate([take(_W_CQ, _W_CKV), take(_W_KR, _W_GA), take(_W_LR, _W_GG), pad,
                          take(_W_CKV, _W_KR), take(_W_GA, _W_LR), take(_W_GG, _W_END)], axis=0)
    o_ref[...] = wt.T.astype(BF16)


def _permute_win(w_in, rows):
    depth, d, n = w_in.shape
    return pl.pallas_call(
        _win_kernel,
        grid=(depth, d // rows),
        in_specs=[pl.BlockSpec((None, n, rows), lambda l, i: (l, 0, i))],
        out_specs=pl.BlockSpec((None, rows, _C_END), lambda l, i: (l, i, 0)),
        out_shape=jax.ShapeDtypeStruct((depth, d, _C_END), BF16),
        name="permute_w_in",
    )(jnp.swapaxes(w_in, 1, 2))


def _front_kernel(h_ref, ln_ref, win_ref, qn_ref, wq_ref, kvn_ref, wkv_ref,
                  wg_ref, bg_ref, tq_ref, tkc_ref, tks_ref,
                  q_ref, k_ref, v_ref, sga_ref, gf_ref, gb_ref, dec_ref, gv_ref, sgg_ref, *, chunk):
    xn = _rms(h_ref[...], ln_ref[...]).astype(BF16)
    u = _dot(xn, win_ref[...])

    def proj(c0, c1):
        return u[:, c0:c1]

    cq_krlr = proj(_C_CQ, _C_CKV)
    krlr = cq_krlr[:, _C_KRLR:_C_CKV]

    cqn = _rms(cq_krlr[:, :_C_KRLR], qn_ref[...]).astype(BF16)
    q_all = _dot(cqn, wq_ref[...])
    q_nope = q_all[:, :MLA_HEADS * MLA_NOPE] * Q_SCALE
    q_rope = q_all[:, MLA_HEADS * MLA_NOPE:]
    tq = tq_ref[...]
    for hd in range(MLA_HEADS):
        g = slice(hd * LANES, (hd + 1) * LANES)
        q_ref[:, hd * QK_PAD:hd * QK_PAD + LANES] = q_nope[:, g].astype(BF16)
        qr = q_rope[:, g] * tq
        q_ref[:, hd * QK_PAD + LANES:(hd + 1) * QK_PAD] = (qr + pltpu.roll(qr, MLA_ROPE, axis=1)).astype(BF16)

    ckvn = _rms(proj(_C_CKV, _C_GA), kvn_ref[...]).astype(BF16)
    kv = _dot(ckvn, wkv_ref[...])
    k_nope = kv[:, :MLA_HEADS * MLA_NOPE]
    v_ref[...] = kv[:, MLA_HEADS * MLA_NOPE:].astype(BF16)
    half = MLA_ROPE // 2
    lane = lax.broadcasted_iota(jnp.int32, krlr.shape, 1)
    kr_sw = jnp.where(lane < half, pltpu.roll(krlr, LANES - half, axis=1), pltpu.roll(krlr, half, axis=1))
    kr = (krlr * tkc_ref[...] + kr_sw * tks_ref[...]).astype(BF16)
    for hd in range(MLA_HEADS):
        k_ref[:, hd * QK_PAD:hd * QK_PAD + LANES] = k_nope[:, hd * LANES:(hd + 1) * LANES].astype(BF16)
        k_ref[:, hd * QK_PAD + LANES:(hd + 1) * QK_PAD] = kr

    sga_ref[...] = _silu(proj(_C_GA, _C_GQ)).astype(BF16)
    sgg_ref[...] = _silu(proj(_C_GG, _C_END)).astype(BF16)

    gq = proj(_C_GQ, _C_GK) * GLA_DK ** -0.5
    gk = proj(_C_GK, _C_GV)
    gv_ref[...] = proj(_C_GV, _C_GG).astype(BF16)
    la = _log2_sigmoid(_dot(krlr.astype(BF16), wg_ref[...]) + bg_ref[...])
    row = lax.broadcasted_iota(jnp.int32, (chunk, chunk), 0)
    col = lax.broadcasted_iota(jnp.int32, (chunk, chunk), 1)
    for d, out_ref in enumerate((gf_ref, gb_ref)):
        forward = d == 0
        tri = jnp.where((col <= row) if forward else (col >= row), 1.0 / GLA_TAU, 0.0).astype(BF16)
        for ci in range(la.shape[0] // chunk):
            r = slice(ci * chunk, (ci + 1) * chunk)
            hi, lo = _split2(la[r, d * GLA_QK:(d + 1) * GLA_QK])
            b = _dot(tri, jnp.concatenate([hi, lo], axis=1))
            b = b[:, :GLA_QK] + b[:, GLA_QK:]
            mid = chunk // 2 - 1 if forward else chunk // 2
            end = chunk - 1 if forward else 0
            b_mid = b[mid:mid + 1, :]
            b_end = b[end:end + 1, :]
            out_ref[r, :GLA_QK] = (gq[r] * jnp.exp2(b - b_mid)).astype(BF16)
            out_ref[r, GLA_QK:] = (gk[r] * jnp.exp2(b_mid - b)).astype(BF16)
            base = d * 3 * GLA_QK
            dec_ref[ci, :, base:base + GLA_QK] = jnp.exp2(b_end)
            dec_ref[ci, :, base + GLA_QK:base + 2 * GLA_QK] = jnp.exp2(b_mid)
            dec_ref[ci, :, base + 2 * GLA_QK:base + 3 * GLA_QK] = jnp.exp2(b_end - b_mid)


def _front_weights(w_in, w_uq, w_ukv, wgf, bgf, wgb, bgb):
    depth = w_in.shape[0]
    half = MLA_ROPE // 2
    swap = lambda w: jnp.concatenate([w[..., half:], w[..., :half]], axis=-1)
    win = _permute_win(w_in, math.gcd(w_in.shape[1], 256))

    wq = w_uq.reshape(depth, MLA_Q_RANK, MLA_HEADS, MLA_NOPE + MLA_ROPE)
    wqn = wq[..., :MLA_NOPE].reshape(depth, MLA_Q_RANK, MLA_HEADS * MLA_NOPE).astype(BF16)
    wr = wq[..., MLA_NOPE:]
    wqr = jnp.concatenate([wr, swap(wr)], axis=-1).reshape(depth, MLA_Q_RANK, MLA_HEADS * LANES).astype(BF16)

    wkv = w_ukv.reshape(depth, MLA_KV_RANK, MLA_HEADS, MLA_NOPE + MLA_V)
    wk = wkv[..., :MLA_NOPE].reshape(depth, MLA_KV_RANK, MLA_HEADS * MLA_NOPE).astype(BF16)
    wv = wkv[..., MLA_NOPE:].reshape(depth, MLA_KV_RANK, MLA_HEADS * MLA_V).astype(BF16)

    wg = jnp.zeros((depth, LANES, 2 * GLA_QK), F32)
    wg = wg.at[:, _LR_LANE:_LR_LANE + GLA_GATE_RANK, :GLA_QK].set(wgf)
    wg = wg.at[:, _LR_LANE + GLA_GATE_RANK:_LR_LANE + 2 * GLA_GATE_RANK, GLA_QK:].set(wgb)
    bg = jnp.concatenate([bgf, bgb], axis=-1).reshape(depth, 1, 2 * GLA_QK)
    wq = jnp.concatenate([wqn, wqr], axis=-1)
    wkv = jnp.concatenate([wk, wv], axis=-1)
    return win, wq, wkv, wg.astype(BF16), bg


def _layer_spec(a, layer):
    return pl.BlockSpec((None,) + a.shape[1:], lambda i: (layer, 0, 0))


def _front(h, ln, weights, qn, kvn, tabs, layer, tm, chunk):
    tokens, d = h.shape
    assert tm % chunk == 0, (tm, chunk)
    win, wq, wkv, wg, bg = weights
    tq, tkc, tks = tabs
    row = lambda n: pl.BlockSpec((tm, n), lambda i: (i, 0))
    full = lambda a: _layer_spec(a, layer)
    ln, qn, kvn = (a.reshape(a.shape[0], 1, a.shape[1]) for a in (ln, qn, kvn))
    outs = [
        (MLA_HEADS * QK_PAD, BF16),
        (MLA_HEADS * QK_PAD, BF16),
        (MLA_WIDTH, BF16),
        (MLA_WIDTH, BF16),
        (2 * GLA_QK, BF16),
        (2 * GLA_QK, BF16),
        None,
        (GLA_WIDTH, BF16),
        (GLA_WIDTH, BF16),
    ]
    dec_spec = pl.BlockSpec((tm // chunk, 1, GLA_DEC), lambda i: (i, 0, 0))
    dec_shape = jax.ShapeDtypeStruct((tokens // chunk, 1, GLA_DEC), F32)
    return pl.pallas_call(
        functools.partial(_front_kernel, chunk=chunk),
        grid=(tokens // tm,),
        in_specs=[row(d), full(ln), full(win), full(qn), full(wq), full(kvn),
                  full(wkv), full(wg), full(bg), row(LANES), row(LANES), row(LANES)],
        out_specs=[dec_spec if o is None else row(o[0]) for o in outs],
        out_shape=[dec_shape if o is None else jax.ShapeDtypeStruct((tokens, o[0]), o[1]) for o in outs],
        compiler_params=pltpu.CompilerParams(dimension_semantics=("arbitrary",), vmem_limit_bytes=VMEM_LIMIT),
        name="front",
    )(h, ln, win, qn, wq, kvn, wkv, wg, bg, tq, tkc, tks)


def _attn_kernel(q_ref, k_ref, v_ref, g_ref, o_ref, s_ref, macc_ref, acc_ref, vext_ref, *, tq):
    vext_ref[:, :MLA_V] = v_ref[0]
    vext_ref[:, MLA_V:] = jnp.ones((v_ref.shape[1], MLA_V), BF16)
    groups = k_ref.shape[1] // LANES
    once = jnp.minimum(pl.program_id(0) + 1, 1)

    @pl.loop(0, q_ref.shape[1] // tq)
    def _(qi):
        rows = pl.ds(pl.multiple_of(qi * tq, tq), tq)

        @pl.loop(0, once)
        def _(_):
            s = _dot_nt(q_ref[0, rows, :], k_ref[0])
            s_ref[:, :s.shape[1]] = s
            m_acc = s[:, :LANES]
            for c in range(1, groups):
                m_acc = jnp.maximum(m_acc, s[:, c * LANES:(c + 1) * LANES])
            macc_ref[...] = m_acc

        m_b = jnp.broadcast_to(jnp.max(macc_ref[...], axis=-1, keepdims=True), (tq, LANES))

        @pl.loop(0, once)
        def _(_):
            cols = [jnp.exp2((s_ref[:, c * LANES:(c + 1) * LANES] - m_b).astype(BF16)) for c in range(groups)]
            acc_ref[...] = _dot(jnp.concatenate(cols, axis=1), vext_ref[...])

        acc = acc_ref[...]
        o_ref[0, rows, :] = (acc[:, :MLA_V] / acc[:, MLA_V:] * g_ref[0, rows, :].astype(F32)).astype(o_ref.dtype)


def _attention(q, k, v, sgate, tq):
    b, s, _ = q.shape
    head = lambda w: pl.BlockSpec((1, s, w), lambda bi, hi: (bi, 0, hi))
    return pl.pallas_call(
        functools.partial(_attn_kernel, tq=tq),
        grid=(b, MLA_HEADS),
        in_specs=[head(QK_PAD), head(QK_PAD), head(MLA_V), head(MLA_V)],
        out_specs=head(MLA_V),
        out_shape=jax.ShapeDtypeStruct((b, s, MLA_WIDTH), BF16),
        scratch_shapes=[pltpu.VMEM((tq, s + LANES), F32), pltpu.VMEM((tq, LANES), F32),
                        pltpu.VMEM((tq, 2 * MLA_V), F32), pltpu.VMEM((s, 2 * MLA_V), BF16)],
        compiler_params=pltpu.CompilerParams(
            dimension_semantics=("arbitrary", "arbitrary"), vmem_limit_bytes=VMEM_LIMIT),
        name="mla_attention",
    )(q, k, v, sgate)


GLA_CHUNKS_PER_STEP = 8


def _gla_direction(g_ref, v_ref, dec_ref, o_ref, state_ref, forward, chunk):
    c = chunk
    n_sub = g_ref.shape[1] // c
    row = lax.broadcasted_iota(jnp.int32, (c, c), 0)
    col = lax.broadcasted_iota(jnp.int32, (c, c), 1)
    tri = (col <= row) if forward else (col >= row)
    lane_head = lax.broadcasted_iota(jnp.int32, (c, LANES), 1) // GLA_DK
    zero = jnp.zeros((c, LANES), BF16)
    base = 0 if forward else 3 * GLA_QK
    for sub in (range(n_sub) if forward else reversed(range(n_sub))):
        r = slice(sub * c, (sub + 1) * c)
        for hd in range(GLA_HEADS):
            pair = (hd // 2) * LANES
            fac = lambda i: dec_ref[sub, :, base + i * GLA_QK + pair:base + i * GLA_QK + pair + LANES]
            q_in = g_ref[0, r, pair:pair + LANES]
            k_in = jnp.where(lane_head == (hd % 2), g_ref[0, r, GLA_QK + pair:GLA_QK + pair + LANES], zero)
            vh = v_ref[0, r, hd * GLA_DV:(hd + 1) * GLA_DV]
            a = jnp.where(tri, _dot_nt(q_in, k_in), 0.0).astype(BF16)
            st = state_ref[hd]
            o = _dot(a, vh) + _dot_nt(q_in, (st * fac(1)).astype(BF16))
            o_ref[0, r, hd * GLA_DV:(hd + 1) * GLA_DV] = o.astype(o_ref.dtype)
            state_ref[hd] = st * fac(0) + _dot_tn(vh, k_in) * fac(2)


def _gla_kernel(gf_ref, vf_ref, df_ref, gb_ref, vb_ref, db_ref, of_ref, ob_ref, sf_ref, sb_ref, *, chunk):
    @pl.when(pl.program_id(1) == 0)
    def _():
        sf_ref[...] = jnp.zeros_like(sf_ref)
        sb_ref[...] = jnp.zeros_like(sb_ref)

    _gla_direction(gf_ref, vf_ref, df_ref, of_ref, sf_ref, True, chunk)
    _gla_direction(gb_ref, vb_ref, db_ref, ob_ref, sb_ref, False, chunk)


def _gla(gf, gb, gv, dec, chunk):
    b, s, _ = gv.shape
    per = math.gcd(s // chunk, GLA_CHUNKS_PER_STEP)
    rows = per * chunk
    n = s // rows
    fwd = lambda w: pl.BlockSpec((1, rows, w), lambda bi, ci: (bi, ci, 0))
    bwd = lambda w: pl.BlockSpec((1, rows, w), lambda bi, ci: (bi, n - 1 - ci, 0))
    dec_fwd = pl.BlockSpec((per, 1, GLA_DEC), lambda bi, ci: (bi * n + ci, 0, 0))
    dec_bwd = pl.BlockSpec((per, 1, GLA_DEC), lambda bi, ci: (bi * n + n - 1 - ci, 0, 0))
    out = jax.ShapeDtypeStruct((b, s, GLA_WIDTH), BF16)
    state = pltpu.VMEM((GLA_HEADS, GLA_DV, LANES), F32)
    return pl.pallas_call(
        functools.partial(_gla_kernel, chunk=chunk),
        grid=(b, n),
        in_specs=[fwd(2 * GLA_QK), fwd(GLA_WIDTH), dec_fwd, bwd(2 * GLA_QK), bwd(GLA_WIDTH), dec_bwd],
        out_specs=[fwd(GLA_WIDTH), bwd(GLA_WIDTH)],
        out_shape=[out, out],
        scratch_shapes=[state, state],
        compiler_params=pltpu.CompilerParams(
            dimension_semantics=("arbitrary", "arbitrary"), vmem_limit_bytes=VMEM_LIMIT),
        name="gla_scan",
    )(gf, gv, dec, gb, gv, dec)


def _back_kernel(h_ref, ym_ref, of_ref, ob_ref, sgg_ref, on_ref, wout_ref, pn_ref, wpg_ref, p_ref, wpp_ref,
                 fn_ref, o_ref, *, last):
    og = of_ref[...].astype(F32) + ob_ref[...].astype(F32)
    sgg = sgg_ref[...].astype(F32)
    on = on_ref[...]
    parts = [ym_ref[...]]
    for hd in range(GLA_HEADS):
        g = slice(hd * GLA_DV, (hd + 1) * GLA_DV)
        parts.append((_rms(og[:, g], on) * sgg[:, g]).astype(BF16))
    h1 = h_ref[...] + _dot(jnp.concatenate(parts, axis=1), wout_ref[...])
    gate = jax.nn.sigmoid(_dot(_rms(h1, pn_ref[...]).astype(BF16), wpg_ref[...]))
    h2 = h1 + gate * _dot(p_ref[0].astype(BF16), wpp_ref[...])
    o_ref[...] = _rms(h2, fn_ref[...]) if last else h2


def _back(h, y_mla, o_f, o_b, sgg, out_norm, w_out, ple_norm, w_pg, p, layer, w_pp, final_norm, tm, last):
    tokens, d = h.shape
    row = lambda n: pl.BlockSpec((tm, n), lambda i: (i, 0))
    full = lambda a: _layer_spec(a, layer)
    p_spec = pl.BlockSpec((1, tm, p.shape[2]), lambda i: (layer, i, 0))
    on, pn = (a.reshape(a.shape[0], 1, a.shape[1]) for a in (out_norm, ple_norm))
    fn = final_norm.reshape(1, -1)
    return pl.pallas_call(
        functools.partial(_back_kernel, last=last),
        grid=(tokens // tm,),
        in_specs=[row(d), row(MLA_WIDTH), row(GLA_WIDTH), row(GLA_WIDTH), row(GLA_WIDTH), full(on),
                  full(w_out), full(pn), full(w_pg), p_spec, full(w_pp),
                  pl.BlockSpec(fn.shape, lambda i: (0, 0))],
        out_specs=row(d),
        out_shape=jax.ShapeDtypeStruct((tokens, d), F32),
        compiler_params=pltpu.CompilerParams(dimension_semantics=("arbitrary",), vmem_limit_bytes=VMEM_LIMIT),
        name="back",
    )(h, y_mla, o_f, o_b, sgg, on, w_out, pn, w_pg, p, w_pp, fn)


def kernel(x, p, positions, ln_mix, w_in, mla_q_norm, w_uq, mla_kv_norm, w_ukv, gla_w_gate_fwd, gla_b_gate_fwd,
           gla_w_gate_bwd, gla_b_gate_bwd, gla_out_norm, w_out, ple_norm, w_ple_gate, w_ple_proj, final_norm):
    batch, seq, d = x.shape
    depth = w_in.shape[0]
    tokens = batch * seq
    t = _tiles(batch, seq)
    tabs = _rope_tables(positions, t["rope_rows"])
    h = x.reshape(tokens, d)
    p_all = p.reshape(depth, tokens, p.shape[-1])
    seq3 = lambda a: a.reshape(batch, seq, a.shape[-1])
    weights = _front_weights(w_in, w_uq, w_ukv, gla_w_gate_fwd, gla_b_gate_fwd, gla_w_gate_bwd, gla_b_gate_bwd)
    w_out, w_ple_gate, w_ple_proj = (w.astype(BF16) for w in (w_out, w_ple_gate, w_ple_proj))
    for i in range(depth):
        q, k, v, sga, gf, gb, dec, gv, sgg = _front(
            h, ln_mix, weights, mla_q_norm, mla_kv_norm, tabs, i, t["row_tile"], t["gla_chunk"])
        y_mla = _attention(seq3(q), seq3(k), seq3(v), seq3(sga), t["q_tile"])
        o_f, o_b = _gla(seq3(gf), seq3(gb), seq3(gv), dec, t["gla_chunk"])
        h = _back(h, y_mla.reshape(tokens, -1), o_f.reshape(tokens, -1), o_b.reshape(tokens, -1), sgg,
                  gla_out_norm, w_out, ple_norm, w_ple_gate, p_all, i,
                  w_ple_proj, final_norm, t["back_tile"], last=(i == depth - 1))
    return h.reshape(batch, seq, d)
```

```python
import functools
import math

import jax
import jax.numpy as jnp
from jax import lax
from jax.experimental import pallas as pl
from jax.experimental.pallas import tpu as pltpu

EPS = 1e-6
MLA_HEADS = 4
MLA_Q_RANK = 384
MLA_KV_RANK = 256
MLA_NOPE = 128
MLA_ROPE = 64
MLA_V = 128
MLA_WIDTH = MLA_HEADS * MLA_V
ROPE_THETA = 10000.0
GLA_HEADS = 4
GLA_DK = 64
GLA_DV = 128
GLA_WIDTH = GLA_HEADS * GLA_DV
GLA_QK = GLA_HEADS * GLA_DK
GLA_GATE_RANK = 16
GLA_TAU = 16.0
GLA_DEC = 2 * 3 * GLA_QK

LANES = 128
QK_PAD = 2 * LANES
VMEM_LIMIT = 48 * 1024 * 1024

Q_SCALE = (MLA_NOPE + MLA_ROPE) ** -0.5 * math.log2(math.e)

F32 = jnp.float32
BF16 = jnp.bfloat16


def _tiles(batch, seq):
    tokens = batch * seq
    row_tile = math.gcd(tokens, 512)
    return dict(
        row_tile=row_tile,
        back_tile=math.gcd(tokens, 1024),
        q_tile=math.gcd(seq, 1024),
        gla_chunk=math.gcd(seq, 256),
        rope_rows=math.gcd(tokens // 4, 1024),
    )


def _dot(a, b):
    return jnp.dot(a, b, preferred_element_type=F32)


def _dot_nt(a, b):
    return lax.dot_general(a, b, (((1,), (1,)), ((), ())), preferred_element_type=F32)


def _dot_tn(a, b):
    return lax.dot_general(a, b, (((0,), (0,)), ((), ())), preferred_element_type=F32)


def _rms(x, g):
    return x * lax.rsqrt(jnp.mean(x * x, axis=-1, keepdims=True) + EPS) * g


def _silu(x):
    return x * jax.nn.sigmoid(x)


def _log2_sigmoid(x):
    y = x * math.log2(math.e)
    return jnp.minimum(y, 0.0) - jnp.log2(1.0 + jnp.exp2(-jnp.abs(y)))


def _split2(x):
    hi = x.astype(BF16)
    lo = (x - hi.astype(F32)).astype(BF16)
    return hi, lo


def _rope_table_kernel(pos_ref, inv_ref, tq_ref, tkc_ref, tks_ref):
    half = MLA_ROPE // 2
    ang = pos_ref[...] * inv_ref[...]
    cos = jnp.cos(ang)
    sin = jnp.sin(ang)
    lane = lax.broadcasted_iota(jnp.int32, cos.shape, 1)
    for i in range(LANES // half):
        place = lambda x, dst: x if dst == i else pltpu.roll(x, ((dst - i) * half) % LANES, axis=1)
        base = jnp.where(lane < half, place(cos, 0),
                         jnp.where(lane < 2 * half, place(cos, 1),
                                   jnp.where(lane < 3 * half, -place(sin, 2), place(sin, 3))))
        tq_ref[i] = base * Q_SCALE
        tkc_ref[i] = jnp.where(lane < 2 * half, base, 0.0)
        tks_ref[i] = jnp.where(lane < 2 * half, pltpu.roll(base, 2 * half, axis=1), 0.0)


def _rope_tables(positions, rows):
    half = MLA_ROPE // 2
    per_row = LANES // half
    tokens = positions.size
    n = tokens // per_row
    inv = ROPE_THETA ** (-jnp.arange(half, dtype=F32) / half)
    pos = jnp.broadcast_to(positions.reshape(per_row, n, 1).astype(F32), (per_row, n, half))
    pos = pos.transpose(1, 0, 2).reshape(n, LANES)
    inv4 = jnp.tile(inv, per_row).reshape(1, LANES)
    out_spec = pl.BlockSpec((per_row, rows, LANES), lambda i: (0, i, 0))
    tabs = pl.pallas_call(
        _rope_table_kernel,
        grid=(n // rows,),
        in_specs=[pl.BlockSpec((rows, LANES), lambda i: (i, 0)), pl.BlockSpec((1, LANES), lambda i: (0, 0))],
        out_specs=[out_spec] * 3,
        out_shape=[jax.ShapeDtypeStruct((per_row, n, LANES), F32)] * 3,
        name="rope_tables",
    )(pos, inv4)
    return tuple(t.reshape(tokens, LANES) for t in tabs)


_C_CQ = 0
_C_KRLR = _C_CQ + MLA_Q_RANK
_C_CKV = _C_KRLR + LANES
_C_GA = _C_CKV + MLA_KV_RANK
_C_GQ = _C_GA + MLA_WIDTH
_C_GK = _C_GQ + GLA_QK
_C_GV = _C_GK + GLA_QK
_C_GG = _C_GV + GLA_WIDTH
_C_END = _C_GG + GLA_WIDTH
_LR_LANE = MLA_ROPE
_W_CQ = 0
_W_CKV = _W_CQ + MLA_Q_RANK
_W_KR = _W_CKV + MLA_KV_RANK
_W_GA = _W_KR + MLA_ROPE
_W_LR = _W_GA + MLA_WIDTH + 2 * GLA_QK + GLA_WIDTH
_W_GG = _W_LR + 2 * GLA_GATE_RANK
_W_END = _W_GG + GLA_WIDTH


def _win_kernel(wt_ref, o_ref):
    take = lambda a, b: wt_ref[a:b, :]
    pad = jnp.zeros((LANES - MLA_ROPE - 2 * GLA_GATE_RANK, wt_ref.shape[1]), wt_ref.dtype)
    wt = jnp.concatenate([take(_W_CQ, _W_CKV), take(_W_KR, _W_GA), take(_W_LR, _W_GG), pad,
                          take(_W_CKV, _W_KR), take(_W_GA, _W_LR), take(_W_GG, _W_END)], axis=0)
    o_ref[...] = wt.T.astype(BF16)


def _permute_win(w_in, rows):
    depth, d, n = w_in.shape
    return pl.pallas_call(
        _win_kernel,
        grid=(depth, d // rows),
        in_specs=[pl.BlockSpec((None, n, rows), lambda l, i: (l, 0, i))],
        out_specs=pl.BlockSpec((None, rows, _C_END), lambda l, i: (l, i, 0)),
        out_shape=jax.ShapeDtypeStruct((depth, d, _C_END), BF16),
        name="permute_w_in",
    )(jnp.swapaxes(w_in, 1, 2))


def _front_kernel(h_ref, ln_ref, win_ref, qn_ref, wq_ref, kvn_ref, wkv_ref,
                  wg_ref, bg_ref, tq_ref, tkc_ref, tks_ref,
                  q_ref, k_ref, v_ref, sga_ref, gf_ref, gb_ref, dec_ref, gv_ref, sgg_ref, *, chunk):
    xn = _rms(h_ref[...], ln_ref[...]).astype(BF16)
    u = _dot(xn, win_ref[...])

    def proj(c0, c1):
        return u[:, c0:c1]

    cq_krlr = proj(_C_CQ, _C_CKV)
    krlr = cq_krlr[:, _C_KRLR:_C_CKV]

    cqn = _rms(cq_krlr[:, :_C_KRLR], qn_ref[...]).astype(BF16)
    q_all = _dot(cqn, wq_ref[...])
    q_nope = q_all[:, :MLA_HEADS * MLA_NOPE] * Q_SCALE
    q_rope = q_all[:, MLA_HEADS * MLA_NOPE:]
    tq = tq_ref[...]
    for hd in range(MLA_HEADS):
        g = slice(hd * LANES, (hd + 1) * LANES)
        q_ref[:, hd * QK_PAD:hd * QK_PAD + LANES] = q_nope[:, g].astype(BF16)
        qr = q_rope[:, g] * tq
        q_ref[:, hd * QK_PAD + LANES:(hd + 1) * QK_PAD] = (qr + pltpu.roll(qr, MLA_ROPE, axis=1)).astype(BF16)

    ckvn = _rms(proj(_C_CKV, _C_GA), kvn_ref[...]).astype(BF16)
    kv = _dot(ckvn, wkv_ref[...])
    k_nope = kv[:, :MLA_HEADS * MLA_NOPE]
    v_ref[...] = kv[:, MLA_HEADS * MLA_NOPE:].astype(BF16)
    half = MLA_ROPE // 2
    lane = lax.broadcasted_iota(jnp.int32, krlr.shape, 1)
    kr_sw = jnp.where(lane < half, pltpu.roll(krlr, LANES - half, axis=1), pltpu.roll(krlr, half, axis=1))
    kr = (krlr * tkc_ref[...] + kr_sw * tks_ref[...]).astype(BF16)
    for hd in range(MLA_HEADS):
        k_ref[:, hd * QK_PAD:hd * QK_PAD + LANES] = k_nope[:, hd * LANES:(hd + 1) * LANES].astype(BF16)
        k_ref[:, hd * QK_PAD + LANES:(hd + 1) * QK_PAD] = kr

    sga_ref[...] = _silu(proj(_C_GA, _C_GQ)).astype(BF16)
    sgg_ref[...] = _silu(proj(_C_GG, _C_END)).astype(BF16)

    gq = proj(_C_GQ, _C_GK) * GLA_DK ** -0.5
    gk = proj(_C_GK, _C_GV)
    gv_ref[...] = proj(_C_GV, _C_GG).astype(BF16)
    la = _log2_sigmoid(_dot(krlr.astype(BF16), wg_ref[...]) + bg_ref[...])
    row = lax.broadcasted_iota(jnp.int32, (chunk, chunk), 0)
    col = lax.broadcasted_iota(jnp.int32, (chunk, chunk), 1)
    for d, out_ref in enumerate((gf_ref, gb_ref)):
        forward = d == 0
        tri = jnp.where((col <= row) if forward else (col >= row), 1.0 / GLA_TAU, 0.0).astype(BF16)
        for ci in range(la.shape[0] // chunk):
            r = slice(ci * chunk, (ci + 1) * chunk)
            hi, lo = _split2(la[r, d * GLA_QK:(d + 1) * GLA_QK])
            b = _dot(tri, jnp.concatenate([hi, lo], axis=1))
            b = b[:, :GLA_QK] + b[:, GLA_QK:]
            mid = chunk // 2 - 1 if forward else chunk // 2
            end = chunk - 1 if forward else 0
            b_mid = b[mid:mid + 1, :]
            b_end = b[end:end + 1, :]
            out_ref[r, :GLA_QK] = (gq[r] * jnp.exp2(b - b_mid)).astype(BF16)
            out_ref[r, GLA_QK:] = (gk[r] * jnp.exp2(b_mid - b)).astype(BF16)
            base = d * 3 * GLA_QK
            dec_ref[ci, :, base:base + GLA_QK] = jnp.exp2(b_end)
            dec_ref[ci, :, base + GLA_QK:base + 2 * GLA_QK] = jnp.exp2(b_mid)
            dec_ref[ci, :, base + 2 * GLA_QK:base + 3 * GLA_QK] = jnp.exp2(b_end - b_mid)


def _front_weights(w_in, w_uq, w_ukv, wgf, bgf, wgb, bgb):
    depth = w_in.shape[0]
    half = MLA_ROPE // 2
    swap = lambda w: jnp.concatenate([w[..., half:], w[..., :half]], axis=-1)
    win = _permute_win(w_in, math.gcd(w_in.shape[1], 256))

    wq = w_uq.reshape(depth, MLA_Q_RANK, MLA_HEADS, MLA_NOPE + MLA_ROPE)
    wqn = wq[..., :MLA_NOPE].reshape(depth, MLA_Q_RANK, MLA_HEADS * MLA_NOPE).astype(BF16)
    wr = wq[..., MLA_NOPE:]
    wqr = jnp.concatenate([wr, swap(wr)], axis=-1).reshape(depth, MLA_Q_RANK, MLA_HEADS * LANES).astype(BF16)

    wkv = w_ukv.reshape(depth, MLA_KV_RANK, MLA_HEADS, MLA_NOPE + MLA_V)
    wk = wkv[..., :MLA_NOPE].reshape(depth, MLA_KV_RANK, MLA_HEADS * MLA_NOPE).astype(BF16)
    wv = wkv[..., MLA_NOPE:].reshape(depth, MLA_KV_RANK, MLA_HEADS * MLA_V).astype(BF16)

    wg = jnp.zeros((depth, LANES, 2 * GLA_QK), F32)
    wg = wg.at[:, _LR_LANE:_LR_LANE + GLA_GATE_RANK, :GLA_QK].set(wgf)
    wg = wg.at[:, _LR_LANE + GLA_GATE_RANK:_LR_LANE + 2 * GLA_GATE_RANK, GLA_QK:].set(wgb)
    bg = jnp.concatenate([bgf, bgb], axis=-1).reshape(depth, 1, 2 * GLA_QK)
    wq = jnp.concatenate([wqn, wqr], axis=-1)
    wkv = jnp.concatenate([wk, wv], axis=-1)
    return win, wq, wkv, wg.astype(BF16), bg


def _layer_spec(a, layer):
    return pl.BlockSpec((None,) + a.shape[1:], lambda i: (layer, 0, 0))


def _front(h, ln, weights, qn, kvn, tabs, layer, tm, chunk):
    tokens, d = h.shape
    assert tm % chunk == 0, (tm, chunk)
    win, wq, wkv, wg, bg = weights
    tq, tkc, tks = tabs
    row = lambda n: pl.BlockSpec((tm, n), lambda i: (i, 0))
    full = lambda a: _layer_spec(a, layer)
    ln, qn, kvn = (a.reshape(a.shape[0], 1, a.shape[1]) for a in (ln, qn, kvn))
    outs = [
        (MLA_HEADS * QK_PAD, BF16),
        (MLA_HEADS * QK_PAD, BF16),
        (MLA_WIDTH, BF16),
        (MLA_WIDTH, BF16),
        (2 * GLA_QK, BF16),
        (2 * GLA_QK, BF16),
        None,
        (GLA_WIDTH, BF16),
        (GLA_WIDTH, BF16),
    ]
    dec_spec = pl.BlockSpec((tm // chunk, 1, GLA_DEC), lambda i: (i, 0, 0))
    dec_shape = jax.ShapeDtypeStruct((tokens // chunk, 1, GLA_DEC), F32)
    return pl.pallas_call(
        functools.partial(_front_kernel, chunk=chunk),
        grid=(tokens // tm,),
        in_specs=[row(d), full(ln), full(win), full(qn), full(wq), full(kvn),
                  full(wkv), full(wg), full(bg), row(LANES), row(LANES), row(LANES)],
        out_specs=[dec_spec if o is None else row(o[0]) for o in outs],
        out_shape=[dec_shape if o is None else jax.ShapeDtypeStruct((tokens, o[0]), o[1]) for o in outs],
        compiler_params=pltpu.CompilerParams(dimension_semantics=("arbitrary",), vmem_limit_bytes=VMEM_LIMIT),
        name="front",
    )(h, ln, win, qn, wq, kvn, wkv, wg, bg, tq, tkc, tks)


def _attn_kernel(q_ref, k_ref, v_ref, g_ref, o_ref, s_ref, macc_ref, acc_ref, vext_ref, *, tq):
    vext_ref[:, :MLA_V] = v_ref[0]
    vext_ref[:, MLA_V:] = jnp.ones((v_ref.shape[1], MLA_V), BF16)
    groups = k_ref.shape[1] // LANES
    once = jnp.minimum(pl.program_id(0) + 1, 1)

    @pl.loop(0, q_ref.shape[1] // tq)
    def _(qi):
        rows = pl.ds(pl.multiple_of(qi * tq, tq), tq)

        @pl.loop(0, once)
        def _(_):
            s = _dot_nt(q_ref[0, rows, :], k_ref[0])
            s_ref[...] = s
            m_acc = s[:, :LANES]
            for c in range(1, groups):
                m_acc = jnp.maximum(m_acc, s[:, c * LANES:(c + 1) * LANES])
            macc_ref[...] = m_acc

        m_b = jnp.broadcast_to(jnp.max(macc_ref[...], axis=-1, keepdims=True), (tq, LANES))

        @pl.loop(0, once)
        def _(_):
            cols = [jnp.exp2((s_ref[:, c * LANES:(c + 1) * LANES] - m_b).astype(BF16)) for c in range(groups)]
            acc_ref[...] = _dot(jnp.concatenate(cols, axis=1), vext_ref[...])

        acc = acc_ref[...]
        o_ref[0, rows, :] = (acc[:, :MLA_V] / acc[:, MLA_V:] * g_ref[0, rows, :].astype(F32)).astype(o_ref.dtype)


def _attention(q, k, v, sgate, tq):
    b, s, _ = q.shape
    head = lambda w: pl.BlockSpec((1, s, w), lambda bi, hi: (bi, 0, hi))
    return pl.pallas_call(
        functools.partial(_attn_kernel, tq=tq),
        grid=(b, MLA_HEADS),
        in_specs=[head(QK_PAD), head(QK_PAD), head(MLA_V), head(MLA_V)],
        out_specs=head(MLA_V),
        out_shape=jax.ShapeDtypeStruct((b, s, MLA_WIDTH), BF16),
        scratch_shapes=[pltpu.VMEM((tq, s), F32), pltpu.VMEM((tq, LANES), F32),
                        pltpu.VMEM((tq, 2 * MLA_V), F32), pltpu.VMEM((s, 2 * MLA_V), BF16)],
        compiler_params=pltpu.CompilerParams(
            dimension_semantics=("arbitrary", "arbitrary"), vmem_limit_bytes=VMEM_LIMIT),
        name="mla_attention",
    )(q, k, v, sgate)


def _gla_direction(g_ref, v_ref, dec_ref, state_ref, forward, chunk, emit):
    c = chunk
    n_sub = g_ref.shape[1] // c
    row = lax.broadcasted_iota(jnp.int32, (c, c), 0)
    col = lax.broadcasted_iota(jnp.int32, (c, c), 1)
    tri = (col <= row) if forward else (col >= row)
    lane_head = lax.broadcasted_iota(jnp.int32, (c, LANES), 1) // GLA_DK
    zero = jnp.zeros((c, LANES), BF16)
    base = 0 if forward else 3 * GLA_QK
    state_ref[...] = jnp.zeros_like(state_ref)
    for sub in (range(n_sub) if forward else reversed(range(n_sub))):
        r = slice(sub * c, (sub + 1) * c)
        for hd in range(GLA_HEADS):
            pair = (hd // 2) * LANES
            fac = lambda i: dec_ref[sub, :, base + i * GLA_QK + pair:base + i * GLA_QK + pair + LANES]
            q_in = g_ref[0, r, pair:pair + LANES]
            k_in = jnp.where(lane_head == (hd % 2), g_ref[0, r, GLA_QK + pair:GLA_QK + pair + LANES], zero)
            vh = v_ref[0, r, hd * GLA_DV:(hd + 1) * GLA_DV]
            a = jnp.where(tri, _dot_nt(q_in, k_in), 0.0).astype(BF16)
            st = state_ref[hd]
            emit(r, hd, _dot(a, vh) + _dot_nt(q_in, (st * fac(1)).astype(BF16)))
            state_ref[hd] = st * fac(0) + _dot_tn(vh, k_in) * fac(2)


def _gla_kernel(gf_ref, gb_ref, v_ref, dec_ref, sgg_ref, on_ref, y_ref, state_ref, *, chunk):
    cols = lambda hd: slice(hd * GLA_DV, (hd + 1) * GLA_DV)

    def keep_forward(r, hd, o):
        y_ref[0, r, cols(hd)] = o.astype(y_ref.dtype)

    def combine(r, hd, o):
        og = y_ref[0, r, cols(hd)].astype(F32) + o
        y_ref[0, r, cols(hd)] = (_rms(og, on_ref[...]) * sgg_ref[0, r, cols(hd)].astype(F32)).astype(y_ref.dtype)

    _gla_direction(gf_ref, v_ref, dec_ref, state_ref, True, chunk, keep_forward)
    _gla_direction(gb_ref, v_ref, dec_ref, state_ref, False, chunk, combine)


def _gla(gf, gb, gv, dec, sgg, out_norm, layer, chunk):
    b, s, _ = gv.shape
    n = s // chunk
    seq = lambda w: pl.BlockSpec((1, s, w), lambda bi: (bi, 0, 0))
    on = out_norm.reshape(out_norm.shape[0], 1, out_norm.shape[1])
    return pl.pallas_call(
        functools.partial(_gla_kernel, chunk=chunk),
        grid=(b,),
        in_specs=[seq(2 * GLA_QK), seq(2 * GLA_QK), seq(GLA_WIDTH),
                  pl.BlockSpec((n, 1, GLA_DEC), lambda bi: (bi, 0, 0)), seq(GLA_WIDTH), _layer_spec(on, layer)],
        out_specs=seq(GLA_WIDTH),
        out_shape=jax.ShapeDtypeStruct((b, s, GLA_WIDTH), BF16),
        scratch_shapes=[pltpu.VMEM((GLA_HEADS, GLA_DV, LANES), F32)],
        compiler_params=pltpu.CompilerParams(dimension_semantics=("arbitrary",), vmem_limit_bytes=VMEM_LIMIT),
        name="gla_scan",
    )(gf, gb, gv, dec, sgg, on)


def _back_kernel(h_ref, ym_ref, yg_ref, wout_ref, pn_ref, wpg_ref, p_ref, wpp_ref, fn_ref, o_ref, *, last):
    y = jnp.concatenate([ym_ref[...], yg_ref[...]], axis=1)
    h1 = h_ref[...] + _dot(y, wout_ref[...])
    gate = jax.nn.sigmoid(_dot(_rms(h1, pn_ref[...]).astype(BF16), wpg_ref[...]))
    h2 = h1 + gate * _dot(p_ref[0].astype(BF16), wpp_ref[...])
    o_ref[...] = _rms(h2, fn_ref[...]) if last else h2


def _back(h, y_mla, y_gla, w_out, ple_norm, w_pg, p, layer, w_pp, final_norm, tm, last):
    tokens, d = h.shape
    row = lambda n: pl.BlockSpec((tm, n), lambda i: (i, 0))
    full = lambda a: _layer_spec(a, layer)
    p_spec = pl.BlockSpec((1, tm, p.shape[2]), lambda i: (layer, i, 0))
    pn = ple_norm.reshape(ple_norm.shape[0], 1, ple_norm.shape[1])
    fn = final_norm.reshape(1, -1)
    return pl.pallas_call(
        functools.partial(_back_kernel, last=last),
        grid=(tokens // tm,),
        in_specs=[row(d), row(MLA_WIDTH), row(GLA_WIDTH), full(w_out), full(pn), full(w_pg), p_spec, full(w_pp),
                  pl.BlockSpec(fn.shape, lambda i: (0, 0))],
        out_specs=row(d),
        out_shape=jax.ShapeDtypeStruct((tokens, d), F32),
        compiler_params=pltpu.CompilerParams(dimension_semantics=("arbitrary",), vmem_limit_bytes=VMEM_LIMIT),
        name="back",
    )(h, y_mla, y_gla, w_out, pn, w_pg, p, w_pp, fn)


def kernel(x, p, positions, ln_mix, w_in, mla_q_norm, w_uq, mla_kv_norm, w_ukv, gla_w_gate_fwd, gla_b_gate_fwd,
           gla_w_gate_bwd, gla_b_gate_bwd, gla_out_norm, w_out, ple_norm, w_ple_gate, w_ple_proj, final_norm):
    batch, seq, d = x.shape
    depth = w_in.shape[0]
    tokens = batch * seq
    t = _tiles(batch, seq)
    tabs = _rope_tables(positions, t["rope_rows"])
    h = x.reshape(tokens, d)
    p_all = p.reshape(depth, tokens, p.shape[-1])
    seq3 = lambda a: a.reshape(batch, seq, a.shape[-1])
    weights = _front_weights(w_in, w_uq, w_ukv, gla_w_gate_fwd, gla_b_gate_fwd, gla_w_gate_bwd, gla_b_gate_bwd)
    w_out, w_ple_gate, w_ple_proj = (w.astype(BF16) for w in (w_out, w_ple_gate, w_ple_proj))
    for i in range(depth):
        q, k, v, sga, gf, gb, dec, gv, sgg = _front(
            h, ln_mix, weights, mla_q_norm, mla_kv_norm, tabs, i, t["row_tile"], t["gla_chunk"])
        y_mla = _attention(seq3(q), seq3(k), seq3(v), seq3(sga), t["q_tile"])
        y_gla = _gla(seq3(gf), seq3(gb), seq3(gv), dec, seq3(sgg), gla_out_norm, i, t["gla_chunk"])
        h = _back(h, y_mla.reshape(tokens, -1), y_gla.reshape(tokens, -1), w_out, ple_norm, w_ple_gate, p_all, i,
                  w_ple_proj, final_norm, t["back_tile"], last=(i == depth - 1))
    return h.reshape(batch, seq, d)
```

```python
import functools
import math

import jax
import jax.numpy as jnp
from jax import lax
from jax.experimental import pallas as pl
from jax.experimental.pallas import tpu as pltpu

EPS = 1e-6
MLA_HEADS = 4
MLA_Q_RANK = 384
MLA_KV_RANK = 256
MLA_NOPE = 128
MLA_ROPE = 64
MLA_V = 128
MLA_WIDTH = MLA_HEADS * MLA_V
ROPE_THETA = 10000.0
GLA_HEADS = 4
GLA_DK = 64
GLA_DV = 128
GLA_WIDTH = GLA_HEADS * GLA_DV
GLA_QK = GLA_HEADS * GLA_DK
GLA_GATE_RANK = 16
GLA_TAU = 16.0
GLA_DEC = 2 * 3 * GLA_QK

LANES = 128
QK_PAD = 2 * LANES
VMEM_LIMIT = 48 * 1024 * 1024

Q_SCALE = (MLA_NOPE + MLA_ROPE) ** -0.5 * math.log2(math.e)

F32 = jnp.float32
BF16 = jnp.bfloat16


def _tiles(batch, seq):
    tokens = batch * seq
    row_tile = math.gcd(tokens, 512)
    return dict(
        row_tile=row_tile,
        back_tile=math.gcd(tokens, 1024),
        q_tile=math.gcd(seq, 1024),
        gla_chunk=math.gcd(seq, 256),
        rope_rows=math.gcd(tokens // 4, 1024),
    )


def _dot(a, b):
    return jnp.dot(a, b, preferred_element_type=F32)


def _dot_nt(a, b):
    return lax.dot_general(a, b, (((1,), (1,)), ((), ())), preferred_element_type=F32)


def _dot_tn(a, b):
    return lax.dot_general(a, b, (((0,), (0,)), ((), ())), preferred_element_type=F32)


def _rms(x, g):
    return x * lax.rsqrt(jnp.mean(x * x, axis=-1, keepdims=True) + EPS) * g


def _silu(x):
    return x * jax.nn.sigmoid(x)


def _log2_sigmoid(x):
    y = x * math.log2(math.e)
    return jnp.minimum(y, 0.0) - jnp.log2(1.0 + jnp.exp2(-jnp.abs(y)))


def _split2(x):
    hi = x.astype(BF16)
    lo = (x - hi.astype(F32)).astype(BF16)
    return hi, lo


def _rope_table_kernel(pos_ref, inv_ref, tq_ref, tkc_ref, tks_ref):
    half = MLA_ROPE // 2
    ang = pos_ref[...] * inv_ref[...]
    cos = jnp.cos(ang)
    sin = jnp.sin(ang)
    lane = lax.broadcasted_iota(jnp.int32, cos.shape, 1)
    for i in range(LANES // half):
        place = lambda x, dst: x if dst == i else pltpu.roll(x, ((dst - i) * half) % LANES, axis=1)
        base = jnp.where(lane < half, place(cos, 0),
                         jnp.where(lane < 2 * half, place(cos, 1),
                                   jnp.where(lane < 3 * half, -place(sin, 2), place(sin, 3))))
        tq_ref[i] = base * Q_SCALE
        tkc_ref[i] = jnp.where(lane < 2 * half, base, 0.0)
        tks_ref[i] = jnp.where(lane < 2 * half, pltpu.roll(base, 2 * half, axis=1), 0.0)


def _rope_tables(positions, rows):
    half = MLA_ROPE // 2
    per_row = LANES // half
    tokens = positions.size
    n = tokens // per_row
    inv = ROPE_THETA ** (-jnp.arange(half, dtype=F32) / half)
    pos = jnp.broadcast_to(positions.reshape(per_row, n, 1).astype(F32), (per_row, n, half))
    pos = pos.transpose(1, 0, 2).reshape(n, LANES)
    inv4 = jnp.tile(inv, per_row).reshape(1, LANES)
    out_spec = pl.BlockSpec((per_row, rows, LANES), lambda i: (0, i, 0))
    tabs = pl.pallas_call(
        _rope_table_kernel,
        grid=(n // rows,),
        in_specs=[pl.BlockSpec((rows, LANES), lambda i: (i, 0)), pl.BlockSpec((1, LANES), lambda i: (0, 0))],
        out_specs=[out_spec] * 3,
        out_shape=[jax.ShapeDtypeStruct((per_row, n, LANES), F32)] * 3,
        name="rope_tables",
    )(pos, inv4)
    return tuple(t.reshape(tokens, LANES) for t in tabs)


_C_CQ = 0
_C_KRLR = _C_CQ + MLA_Q_RANK
_C_CKV = _C_KRLR + LANES
_C_GA = _C_CKV + MLA_KV_RANK
_C_GQ = _C_GA + MLA_WIDTH
_C_GK = _C_GQ + GLA_QK
_C_GV = _C_GK + GLA_QK
_C_GG = _C_GV + GLA_WIDTH
_C_END = _C_GG + GLA_WIDTH
_LR_LANE = MLA_ROPE
_W_CQ = 0
_W_CKV = _W_CQ + MLA_Q_RANK
_W_KR = _W_CKV + MLA_KV_RANK
_W_GA = _W_KR + MLA_ROPE
_W_LR = _W_GA + MLA_WIDTH + 2 * GLA_QK + GLA_WIDTH
_W_GG = _W_LR + 2 * GLA_GATE_RANK
_W_END = _W_GG + GLA_WIDTH


def _win_kernel(wt_ref, o_ref):
    take = lambda a, b: wt_ref[a:b, :]
    pad = jnp.zeros((LANES - MLA_ROPE - 2 * GLA_GATE_RANK, wt_ref.shape[1]), wt_ref.dtype)
    wt = jnp.concatenate([take(_W_CQ, _W_CKV), take(_W_KR, _W_GA), take(_W_LR, _W_GG), pad,
                          take(_W_CKV, _W_KR), take(_W_GA, _W_LR), take(_W_GG, _W_END)], axis=0)
    o_ref[...] = wt.T.astype(BF16)


def _permute_win(w_in, rows):
    depth, d, n = w_in.shape
    return pl.pallas_call(
        _win_kernel,
        grid=(depth, d // rows),
        in_specs=[pl.BlockSpec((None, n, rows), lambda l, i: (l, 0, i))],
        out_specs=pl.BlockSpec((None, rows, _C_END), lambda l, i: (l, i, 0)),
        out_shape=jax.ShapeDtypeStruct((depth, d, _C_END), BF16),
        name="permute_w_in",
    )(jnp.swapaxes(w_in, 1, 2))


def _front_kernel(h_ref, ln_ref, win_ref, qn_ref, wq_ref, kvn_ref, wkv_ref,
                  wg_ref, bg_ref, tq_ref, tkc_ref, tks_ref,
                  q_ref, k_ref, v_ref, sga_ref, gf_ref, gb_ref, dec_ref, gv_ref, sgg_ref, *, chunk):
    xn = _rms(h_ref[...], ln_ref[...]).astype(BF16)
    u = _dot(xn, win_ref[...])

    def proj(c0, c1):
        return u[:, c0:c1]

    cq_krlr = proj(_C_CQ, _C_CKV)
    krlr = cq_krlr[:, _C_KRLR:_C_CKV]

    cqn = _rms(cq_krlr[:, :_C_KRLR], qn_ref[...]).astype(BF16)
    q_all = _dot(cqn, wq_ref[...])
    q_nope = q_all[:, :MLA_HEADS * MLA_NOPE] * Q_SCALE
    q_rope = q_all[:, MLA_HEADS * MLA_NOPE:]
    tq = tq_ref[...]
    for hd in range(MLA_HEADS):
        g = slice(hd * LANES, (hd + 1) * LANES)
        q_ref[:, hd * QK_PAD:hd * QK_PAD + LANES] = q_nope[:, g].astype(BF16)
        qr = q_rope[:, g] * tq
        q_ref[:, hd * QK_PAD + LANES:(hd + 1) * QK_PAD] = (qr + pltpu.roll(qr, MLA_ROPE, axis=1)).astype(BF16)

    ckvn = _rms(proj(_C_CKV, _C_GA), kvn_ref[...]).astype(BF16)
    kv = _dot(ckvn, wkv_ref[...])
    k_nope = kv[:, :MLA_HEADS * MLA_NOPE]
    v_ref[...] = kv[:, MLA_HEADS * MLA_NOPE:].astype(BF16)
    half = MLA_ROPE // 2
    lane = lax.broadcasted_iota(jnp.int32, krlr.shape, 1)
    kr_sw = jnp.where(lane < half, pltpu.roll(krlr, LANES - half, axis=1), pltpu.roll(krlr, half, axis=1))
    kr = (krlr * tkc_ref[...] + kr_sw * tks_ref[...]).astype(BF16)
    for hd in range(MLA_HEADS):
        k_ref[:, hd * QK_PAD:hd * QK_PAD + LANES] = k_nope[:, hd * LANES:(hd + 1) * LANES].astype(BF16)
        k_ref[:, hd * QK_PAD + LANES:(hd + 1) * QK_PAD] = kr

    sga_ref[...] = _silu(proj(_C_GA, _C_GQ)).astype(BF16)
    sgg_ref[...] = _silu(proj(_C_GG, _C_END)).astype(BF16)

    gq = proj(_C_GQ, _C_GK) * GLA_DK ** -0.5
    gk = proj(_C_GK, _C_GV)
    gv_ref[...] = proj(_C_GV, _C_GG).astype(BF16)
    la = _log2_sigmoid(_dot(krlr.astype(BF16), wg_ref[...]) + bg_ref[...])
    row = lax.broadcasted_iota(jnp.int32, (chunk, chunk), 0)
    col = lax.broadcasted_iota(jnp.int32, (chunk, chunk), 1)
    for d, out_ref in enumerate((gf_ref, gb_ref)):
        forward = d == 0
        tri = jnp.where((col <= row) if forward else (col >= row), 1.0 / GLA_TAU, 0.0).astype(BF16)
        for ci in range(la.shape[0] // chunk):
            r = slice(ci * chunk, (ci + 1) * chunk)
            hi, lo = _split2(la[r, d * GLA_QK:(d + 1) * GLA_QK])
            b = _dot(tri, jnp.concatenate([hi, lo], axis=1))
            b = b[:, :GLA_QK] + b[:, GLA_QK:]
            mid = chunk // 2 - 1 if forward else chunk // 2
            end = chunk - 1 if forward else 0
            b_mid = b[mid:mid + 1, :]
            b_end = b[end:end + 1, :]
            out_ref[r, :GLA_QK] = (gq[r] * jnp.exp2(b - b_mid)).astype(BF16)
            out_ref[r, GLA_QK:] = (gk[r] * jnp.exp2(b_mid - b)).astype(BF16)
            base = d * 3 * GLA_QK
            dec_ref[ci, :, base:base + GLA_QK] = jnp.exp2(b_end)
            dec_ref[ci, :, base + GLA_QK:base + 2 * GLA_QK] = jnp.exp2(b_mid)
            dec_ref[ci, :, base + 2 * GLA_QK:base + 3 * GLA_QK] = jnp.exp2(b_end - b_mid)


def _front_weights(w_in, w_uq, w_ukv, wgf, bgf, wgb, bgb):
    depth = w_in.shape[0]
    half = MLA_ROPE // 2
    swap = lambda w: jnp.concatenate([w[..., half:], w[..., :half]], axis=-1)
    win = _permute_win(w_in, math.gcd(w_in.shape[1], 256))

    wq = w_uq.reshape(depth, MLA_Q_RANK, MLA_HEADS, MLA_NOPE + MLA_ROPE)
    wqn = wq[..., :MLA_NOPE].reshape(depth, MLA_Q_RANK, MLA_HEADS * MLA_NOPE).astype(BF16)
    wr = wq[..., MLA_NOPE:]
    wqr = jnp.concatenate([wr, swap(wr)], axis=-1).reshape(depth, MLA_Q_RANK, MLA_HEADS * LANES).astype(BF16)

    wkv = w_ukv.reshape(depth, MLA_KV_RANK, MLA_HEADS, MLA_NOPE + MLA_V)
    wk = wkv[..., :MLA_NOPE].reshape(depth, MLA_KV_RANK, MLA_HEADS * MLA_NOPE).astype(BF16)
    wv = wkv[..., MLA_NOPE:].reshape(depth, MLA_KV_RANK, MLA_HEADS * MLA_V).astype(BF16)

    wg = jnp.zeros((depth, LANES, 2 * GLA_QK), F32)
    wg = wg.at[:, _LR_LANE:_LR_LANE + GLA_GATE_RANK, :GLA_QK].set(wgf)
    wg = wg.at[:, _LR_LANE + GLA_GATE_RANK:_LR_LANE + 2 * GLA_GATE_RANK, GLA_QK:].set(wgb)
    bg = jnp.concatenate([bgf, bgb], axis=-1).reshape(depth, 1, 2 * GLA_QK)
    wq = jnp.concatenate([wqn, wqr], axis=-1)
    wkv = jnp.concatenate([wk, wv], axis=-1)
    return win, wq, wkv, wg.astype(BF16), bg


def _layer_spec(a, layer):
    return pl.BlockSpec((None,) + a.shape[1:], lambda i: (layer, 0, 0))


def _front(h, ln, weights, qn, kvn, tabs, layer, tm, chunk):
    tokens, d = h.shape
    assert tm % chunk == 0, (tm, chunk)
    win, wq, wkv, wg, bg = weights
    tq, tkc, tks = tabs
    row = lambda n: pl.BlockSpec((tm, n), lambda i: (i, 0))
    full = lambda a: _layer_spec(a, layer)
    ln, qn, kvn = (a.reshape(a.shape[0], 1, a.shape[1]) for a in (ln, qn, kvn))
    outs = [
        (MLA_HEADS * QK_PAD, BF16),
        (MLA_HEADS * QK_PAD, BF16),
        (MLA_WIDTH, BF16),
        (MLA_WIDTH, BF16),
        (2 * GLA_QK, BF16),
        (2 * GLA_QK, BF16),
        None,
        (GLA_WIDTH, BF16),
        (GLA_WIDTH, BF16),
    ]
    dec_spec = pl.BlockSpec((tm // chunk, 1, GLA_DEC), lambda i: (i, 0, 0))
    dec_shape = jax.ShapeDtypeStruct((tokens // chunk, 1, GLA_DEC), F32)
    return pl.pallas_call(
        functools.partial(_front_kernel, chunk=chunk),
        grid=(tokens // tm,),
        in_specs=[row(d), full(ln), full(win), full(qn), full(wq), full(kvn),
                  full(wkv), full(wg), full(bg), row(LANES), row(LANES), row(LANES)],
        out_specs=[dec_spec if o is None else row(o[0]) for o in outs],
        out_shape=[dec_shape if o is None else jax.ShapeDtypeStruct((tokens, o[0]), o[1]) for o in outs],
        compiler_params=pltpu.CompilerParams(dimension_semantics=("arbitrary",), vmem_limit_bytes=VMEM_LIMIT),
        name="front",
    )(h, ln, win, qn, wq, kvn, wkv, wg, bg, tq, tkc, tks)


def _attn_kernel(q_ref, k_ref, v_ref, g_ref, o_ref, s_ref, m_ref, vext_ref, *, tq):
    vext_ref[:, :MLA_V] = v_ref[0]
    vext_ref[:, MLA_V:] = jnp.ones((v_ref.shape[1], MLA_V), BF16)
    groups = k_ref.shape[1] // LANES
    once = jnp.minimum(pl.program_id(0) + 1, 1)

    @pl.loop(0, q_ref.shape[1] // tq)
    def _(qi):
        rows = pl.ds(pl.multiple_of(qi * tq, tq), tq)

        @pl.loop(0, once)
        def _(_):
            s = _dot_nt(q_ref[0, rows, :], k_ref[0])
            s_ref[...] = s
            m_acc = s[:, :LANES]
            for c in range(1, groups):
                m_acc = jnp.maximum(m_acc, s[:, c * LANES:(c + 1) * LANES])
            m_ref[...] = jnp.broadcast_to(jnp.max(m_acc, axis=-1, keepdims=True), (tq, LANES))

        m_b = m_ref[...]

        @pl.loop(0, once)
        def _(_):
            cols = [jnp.exp2((s_ref[:, c * LANES:(c + 1) * LANES] - m_b).astype(BF16)) for c in range(groups)]
            acc = _dot(jnp.concatenate(cols, axis=1), vext_ref[...])
            o_ref[0, rows, :] = (acc[:, :MLA_V] / acc[:, MLA_V:] * g_ref[0, rows, :].astype(F32)).astype(o_ref.dtype)


def _attention(q, k, v, sgate, tq):
    b, s, _ = q.shape
    head = lambda w: pl.BlockSpec((1, s, w), lambda bi, hi: (bi, 0, hi))
    return pl.pallas_call(
        functools.partial(_attn_kernel, tq=tq),
        grid=(b, MLA_HEADS),
        in_specs=[head(QK_PAD), head(QK_PAD), head(MLA_V), head(MLA_V)],
        out_specs=head(MLA_V),
        out_shape=jax.ShapeDtypeStruct((b, s, MLA_WIDTH), BF16),
        scratch_shapes=[pltpu.VMEM((tq, s), F32), pltpu.VMEM((tq, LANES), F32), pltpu.VMEM((s, 2 * MLA_V), BF16)],
        compiler_params=pltpu.CompilerParams(
            dimension_semantics=("arbitrary", "arbitrary"), vmem_limit_bytes=VMEM_LIMIT),
        name="mla_attention",
    )(q, k, v, sgate)


def _gla_direction(g_ref, v_ref, dec_ref, state_ref, forward, chunk, emit):
    c = chunk
    n_sub = g_ref.shape[1] // c
    row = lax.broadcasted_iota(jnp.int32, (c, c), 0)
    col = lax.broadcasted_iota(jnp.int32, (c, c), 1)
    tri = (col <= row) if forward else (col >= row)
    lane_head = lax.broadcasted_iota(jnp.int32, (c, LANES), 1) // GLA_DK
    zero = jnp.zeros((c, LANES), BF16)
    base = 0 if forward else 3 * GLA_QK
    state_ref[...] = jnp.zeros_like(state_ref)
    for sub in (range(n_sub) if forward else reversed(range(n_sub))):
        r = slice(sub * c, (sub + 1) * c)
        for hd in range(GLA_HEADS):
            pair = (hd // 2) * LANES
            fac = lambda i: dec_ref[sub, :, base + i * GLA_QK + pair:base + i * GLA_QK + pair + LANES]
            q_in = g_ref[0, r, pair:pair + LANES]
            k_in = jnp.where(lane_head == (hd % 2), g_ref[0, r, GLA_QK + pair:GLA_QK + pair + LANES], zero)
            vh = v_ref[0, r, hd * GLA_DV:(hd + 1) * GLA_DV]
            a = jnp.where(tri, _dot_nt(q_in, k_in), 0.0).astype(BF16)
            st = state_ref[hd]
            emit(r, hd, _dot(a, vh) + _dot_nt(q_in, (st * fac(1)).astype(BF16)))
            state_ref[hd] = st * fac(0) + _dot_tn(vh, k_in) * fac(2)


def _gla_kernel(gf_ref, gb_ref, v_ref, dec_ref, sgg_ref, on_ref, y_ref, state_ref, *, chunk):
    cols = lambda hd: slice(hd * GLA_DV, (hd + 1) * GLA_DV)

    def keep_forward(r, hd, o):
        y_ref[0, r, cols(hd)] = o.astype(y_ref.dtype)

    def combine(r, hd, o):
        og = y_ref[0, r, cols(hd)].astype(F32) + o
        y_ref[0, r, cols(hd)] = (_rms(og, on_ref[...]) * sgg_ref[0, r, cols(hd)].astype(F32)).astype(y_ref.dtype)

    _gla_direction(gf_ref, v_ref, dec_ref, state_ref, True, chunk, keep_forward)
    _gla_direction(gb_ref, v_ref, dec_ref, state_ref, False, chunk, combine)


def _gla(gf, gb, gv, dec, sgg, out_norm, layer, chunk):
    b, s, _ = gv.shape
    n = s // chunk
    seq = lambda w: pl.BlockSpec((1, s, w), lambda bi: (bi, 0, 0))
    on = out_norm.reshape(out_norm.shape[0], 1, out_norm.shape[1])
    return pl.pallas_call(
        functools.partial(_gla_kernel, chunk=chunk),
        grid=(b,),
        in_specs=[seq(2 * GLA_QK), seq(2 * GLA_QK), seq(GLA_WIDTH),
                  pl.BlockSpec((n, 1, GLA_DEC), lambda bi: (bi, 0, 0)), seq(GLA_WIDTH), _layer_spec(on, layer)],
        out_specs=seq(GLA_WIDTH),
        out_shape=jax.ShapeDtypeStruct((b, s, GLA_WIDTH), BF16),
        scratch_shapes=[pltpu.VMEM((GLA_HEADS, GLA_DV, LANES), F32)],
        compiler_params=pltpu.CompilerParams(dimension_semantics=("arbitrary",), vmem_limit_bytes=VMEM_LIMIT),
        name="gla_scan",
    )(gf, gb, gv, dec, sgg, on)


def _back_kernel(h_ref, ym_ref, yg_ref, wout_ref, pn_ref, wpg_ref, p_ref, wpp_ref, fn_ref, o_ref, *, last):
    y = jnp.concatenate([ym_ref[...], yg_ref[...]], axis=1)
    h1 = h_ref[...] + _dot(y, wout_ref[...])
    gate = jax.nn.sigmoid(_dot(_rms(h1, pn_ref[...]).astype(BF16), wpg_ref[...]))
    h2 = h1 + gate * _dot(p_ref[0].astype(BF16), wpp_ref[...])
    o_ref[...] = _rms(h2, fn_ref[...]) if last else h2


def _back(h, y_mla, y_gla, w_out, ple_norm, w_pg, p, layer, w_pp, final_norm, tm, last):
    tokens, d = h.shape
    row = lambda n: pl.BlockSpec((tm, n), lambda i: (i, 0))
    full = lambda a: _layer_spec(a, layer)
    p_spec = pl.BlockSpec((1, tm, p.shape[2]), lambda i: (layer, i, 0))
    pn = ple_norm.reshape(ple_norm.shape[0], 1, ple_norm.shape[1])
    fn = final_norm.reshape(1, -1)
    return pl.pallas_call(
        functools.partial(_back_kernel, last=last),
        grid=(tokens // tm,),
        in_specs=[row(d), row(MLA_WIDTH), row(GLA_WIDTH), full(w_out), full(pn), full(w_pg), p_spec, full(w_pp),
                  pl.BlockSpec(fn.shape, lambda i: (0, 0))],
        out_specs=row(d),
        out_shape=jax.ShapeDtypeStruct((tokens, d), F32),
        compiler_params=pltpu.CompilerParams(dimension_semantics=("arbitrary",), vmem_limit_bytes=VMEM_LIMIT),
        name="back",
    )(h, y_mla, y_gla, w_out, pn, w_pg, p, w_pp, fn)


def kernel(x, p, positions, ln_mix, w_in, mla_q_norm, w_uq, mla_kv_norm, w_ukv, gla_w_gate_fwd, gla_b_gate_fwd,
           gla_w_gate_bwd, gla_b_gate_bwd, gla_out_norm, w_out, ple_norm, w_ple_gate, w_ple_proj, final_norm):
    batch, seq, d = x.shape
    depth = w_in.shape[0]
    tokens = batch * seq
    t = _tiles(batch, seq)
    tabs = _rope_tables(positions, t["rope_rows"])
    h = x.reshape(tokens, d)
    p_all = p.reshape(depth, tokens, p.shape[-1])
    seq3 = lambda a: a.reshape(batch, seq, a.shape[-1])
    weights = _front_weights(w_in, w_uq, w_ukv, gla_w_gate_fwd, gla_b_gate_fwd, gla_w_gate_bwd, gla_b_gate_bwd)
    w_out, w_ple_gate, w_ple_proj = (w.astype(BF16) for w in (w_out, w_ple_gate, w_ple_proj))
    for i in range(depth):
        q, k, v, sga, gf, gb, dec, gv, sgg = _front(
            h, ln_mix, weights, mla_q_norm, mla_kv_norm, tabs, i, t["row_tile"], t["gla_chunk"])
        y_mla = _attention(seq3(q), seq3(k), seq3(v), seq3(sga), t["q_tile"])
        y_gla = _gla(seq3(gf), seq3(gb), seq3(gv), dec, seq3(sgg), gla_out_norm, i, t["gla_chunk"])
        h = _back(h, y_mla.reshape(tokens, -1), y_gla.reshape(tokens, -1), w_out, ple_norm, w_ple_gate, p_all, i,
                  w_ple_proj, final_norm, t["back_tile"], last=(i == depth - 1))
    return h.reshape(batch, seq, d)
```

```python
import functools
import math

import jax
import jax.numpy as jnp
from jax import lax
from jax.experimental import pallas as pl
from jax.experimental.pallas import tpu as pltpu

EPS = 1e-6
MLA_HEADS = 4
MLA_Q_RANK = 384
MLA_KV_RANK = 256
MLA_NOPE = 128
MLA_ROPE = 64
MLA_V = 128
MLA_WIDTH = MLA_HEADS * MLA_V
ROPE_THETA = 10000.0
GLA_HEADS = 4
GLA_DK = 64
GLA_DV = 128
GLA_WIDTH = GLA_HEADS * GLA_DV
GLA_QK = GLA_HEADS * GLA_DK
GLA_GATE_RANK = 16
GLA_TAU = 16.0
GLA_DEC = 2 * 3 * GLA_QK

LANES = 128
QK_PAD = 2 * LANES
VMEM_LIMIT = 48 * 1024 * 1024

Q_SCALE = (MLA_NOPE + MLA_ROPE) ** -0.5 * math.log2(math.e)

F32 = jnp.float32
BF16 = jnp.bfloat16


def _tiles(batch, seq):
    tokens = batch * seq
    row_tile = math.gcd(tokens, 512)
    return dict(
        row_tile=row_tile,
        back_tile=math.gcd(tokens, 1024),
        q_tile=math.gcd(seq, 1024),
        gla_chunk=math.gcd(seq, 128),
        rope_rows=math.gcd(tokens // 4, 1024),
    )


def _dot(a, b):
    return jnp.dot(a, b, preferred_element_type=F32)


def _dot_nt(a, b):
    return lax.dot_general(a, b, (((1,), (1,)), ((), ())), preferred_element_type=F32)


def _dot_tn(a, b):
    return lax.dot_general(a, b, (((0,), (0,)), ((), ())), preferred_element_type=F32)


def _rms(x, g):
    return x * lax.rsqrt(jnp.mean(x * x, axis=-1, keepdims=True) + EPS) * g


def _silu(x):
    return x * jax.nn.sigmoid(x)


def _log2_sigmoid(x):
    y = x * math.log2(math.e)
    return jnp.minimum(y, 0.0) - jnp.log2(1.0 + jnp.exp2(-jnp.abs(y)))


def _split2(x):
    hi = x.astype(BF16)
    lo = (x - hi.astype(F32)).astype(BF16)
    return hi, lo


def _rope_table_kernel(pos_ref, inv_ref, tq_ref, tkc_ref, tks_ref):
    half = MLA_ROPE // 2
    ang = pos_ref[...] * inv_ref[...]
    cos = jnp.cos(ang)
    sin = jnp.sin(ang)
    lane = lax.broadcasted_iota(jnp.int32, cos.shape, 1)
    for i in range(LANES // half):
        place = lambda x, dst: x if dst == i else pltpu.roll(x, ((dst - i) * half) % LANES, axis=1)
        base = jnp.where(lane < half, place(cos, 0),
                         jnp.where(lane < 2 * half, place(cos, 1),
                                   jnp.where(lane < 3 * half, -place(sin, 2), place(sin, 3))))
        tq_ref[i] = base * Q_SCALE
        tkc_ref[i] = jnp.where(lane < 2 * half, base, 0.0)
        tks_ref[i] = jnp.where(lane < 2 * half, pltpu.roll(base, 2 * half, axis=1), 0.0)


def _rope_tables(positions, rows):
    half = MLA_ROPE // 2
    per_row = LANES // half
    tokens = positions.size
    n = tokens // per_row
    inv = ROPE_THETA ** (-jnp.arange(half, dtype=F32) / half)
    pos = jnp.broadcast_to(positions.reshape(per_row, n, 1).astype(F32), (per_row, n, half))
    pos = pos.transpose(1, 0, 2).reshape(n, LANES)
    inv4 = jnp.tile(inv, per_row).reshape(1, LANES)
    out_spec = pl.BlockSpec((per_row, rows, LANES), lambda i: (0, i, 0))
    tabs = pl.pallas_call(
        _rope_table_kernel,
        grid=(n // rows,),
        in_specs=[pl.BlockSpec((rows, LANES), lambda i: (i, 0)), pl.BlockSpec((1, LANES), lambda i: (0, 0))],
        out_specs=[out_spec] * 3,
        out_shape=[jax.ShapeDtypeStruct((per_row, n, LANES), F32)] * 3,
        name="rope_tables",
    )(pos, inv4)
    return tuple(t.reshape(tokens, LANES) for t in tabs)


_C_CQ = 0
_C_KRLR = _C_CQ + MLA_Q_RANK
_C_CKV = _C_KRLR + LANES
_C_GA = _C_CKV + MLA_KV_RANK
_C_GQ = _C_GA + MLA_WIDTH
_C_GK = _C_GQ + GLA_QK
_C_GV = _C_GK + GLA_QK
_C_GG = _C_GV + GLA_WIDTH
_C_END = _C_GG + GLA_WIDTH
_LR_LANE = MLA_ROPE
_W_CQ = 0
_W_CKV = _W_CQ + MLA_Q_RANK
_W_KR = _W_CKV + MLA_KV_RANK
_W_GA = _W_KR + MLA_ROPE
_W_LR = _W_GA + MLA_WIDTH + 2 * GLA_QK + GLA_WIDTH
_W_GG = _W_LR + 2 * GLA_GATE_RANK
_W_END = _W_GG + GLA_WIDTH


def _win_kernel(wt_ref, o_ref):
    take = lambda a, b: wt_ref[a:b, :]
    pad = jnp.zeros((LANES - MLA_ROPE - 2 * GLA_GATE_RANK, wt_ref.shape[1]), wt_ref.dtype)
    wt = jnp.concatenate([take(_W_CQ, _W_CKV), take(_W_KR, _W_GA), take(_W_LR, _W_GG), pad,
                          take(_W_CKV, _W_KR), take(_W_GA, _W_LR), take(_W_GG, _W_END)], axis=0)
    o_ref[...] = wt.T.astype(BF16)


def _permute_win(w_in, rows):
    depth, d, n = w_in.shape
    return pl.pallas_call(
        _win_kernel,
        grid=(depth, d // rows),
        in_specs=[pl.BlockSpec((None, n, rows), lambda l, i: (l, 0, i))],
        out_specs=pl.BlockSpec((None, rows, _C_END), lambda l, i: (l, i, 0)),
        out_shape=jax.ShapeDtypeStruct((depth, d, _C_END), BF16),
        name="permute_w_in",
    )(jnp.swapaxes(w_in, 1, 2))


def _front_kernel(h_ref, ln_ref, win_ref, qn_ref, wq_ref, kvn_ref, wkv_ref,
                  wg_ref, bg_ref, tq_ref, tkc_ref, tks_ref,
                  q_ref, k_ref, v_ref, sga_ref, gf_ref, gb_ref, dec_ref, gv_ref, sgg_ref, *, chunk):
    xn = _rms(h_ref[...], ln_ref[...]).astype(BF16)
    u = _dot(xn, win_ref[...])

    def proj(c0, c1):
        return u[:, c0:c1]

    cq_krlr = proj(_C_CQ, _C_CKV)
    krlr = cq_krlr[:, _C_KRLR:_C_CKV]

    cqn = _rms(cq_krlr[:, :_C_KRLR], qn_ref[...]).astype(BF16)
    q_all = _dot(cqn, wq_ref[...])
    q_nope = q_all[:, :MLA_HEADS * MLA_NOPE] * Q_SCALE
    q_rope = q_all[:, MLA_HEADS * MLA_NOPE:]
    tq = tq_ref[...]
    for hd in range(MLA_HEADS):
        g = slice(hd * LANES, (hd + 1) * LANES)
        q_ref[:, hd * QK_PAD:hd * QK_PAD + LANES] = q_nope[:, g].astype(BF16)
        qr = q_rope[:, g] * tq
        q_ref[:, hd * QK_PAD + LANES:(hd + 1) * QK_PAD] = (qr + pltpu.roll(qr, MLA_ROPE, axis=1)).astype(BF16)

    ckvn = _rms(proj(_C_CKV, _C_GA), kvn_ref[...]).astype(BF16)
    kv = _dot(ckvn, wkv_ref[...])
    k_nope = kv[:, :MLA_HEADS * MLA_NOPE]
    v_ref[...] = kv[:, MLA_HEADS * MLA_NOPE:].astype(BF16)
    half = MLA_ROPE // 2
    lane = lax.broadcasted_iota(jnp.int32, krlr.shape, 1)
    kr_sw = jnp.where(lane < half, pltpu.roll(krlr, LANES - half, axis=1), pltpu.roll(krlr, half, axis=1))
    kr = (krlr * tkc_ref[...] + kr_sw * tks_ref[...]).astype(BF16)
    for hd in range(MLA_HEADS):
        k_ref[:, hd * QK_PAD:hd * QK_PAD + LANES] = k_nope[:, hd * LANES:(hd + 1) * LANES].astype(BF16)
        k_ref[:, hd * QK_PAD + LANES:(hd + 1) * QK_PAD] = kr

    sga_ref[...] = _silu(proj(_C_GA, _C_GQ)).astype(BF16)
    sgg_ref[...] = _silu(proj(_C_GG, _C_END)).astype(BF16)

    gq = proj(_C_GQ, _C_GK) * GLA_DK ** -0.5
    gk = proj(_C_GK, _C_GV)
    gv_ref[...] = proj(_C_GV, _C_GG).astype(BF16)
    la = _log2_sigmoid(_dot(krlr.astype(BF16), wg_ref[...]) + bg_ref[...])
    row = lax.broadcasted_iota(jnp.int32, (chunk, chunk), 0)
    col = lax.broadcasted_iota(jnp.int32, (chunk, chunk), 1)
    for d, out_ref in enumerate((gf_ref, gb_ref)):
        forward = d == 0
        tri = jnp.where((col <= row) if forward else (col >= row), 1.0 / GLA_TAU, 0.0).astype(BF16)
        for ci in range(la.shape[0] // chunk):
            r = slice(ci * chunk, (ci + 1) * chunk)
            hi, lo = _split2(la[r, d * GLA_QK:(d + 1) * GLA_QK])
            b = _dot(tri, jnp.concatenate([hi, lo], axis=1))
            b = b[:, :GLA_QK] + b[:, GLA_QK:]
            mid = chunk // 2 - 1 if forward else chunk // 2
            end = chunk - 1 if forward else 0
            b_mid = b[mid:mid + 1, :]
            b_end = b[end:end + 1, :]
            out_ref[r, :GLA_QK] = (gq[r] * jnp.exp2(b - b_mid)).astype(BF16)
            out_ref[r, GLA_QK:] = (gk[r] * jnp.exp2(b_mid - b)).astype(BF16)
            base = d * 3 * GLA_QK
            dec_ref[ci, :, base:base + GLA_QK] = jnp.exp2(b_end)
            dec_ref[ci, :, base + GLA_QK:base + 2 * GLA_QK] = jnp.exp2(b_mid)
            dec_ref[ci, :, base + 2 * GLA_QK:base + 3 * GLA_QK] = jnp.exp2(b_end - b_mid)


def _front_weights(w_in, w_uq, w_ukv, wgf, bgf, wgb, bgb):
    depth = w_in.shape[0]
    half = MLA_ROPE // 2
    swap = lambda w: jnp.concatenate([w[..., half:], w[..., :half]], axis=-1)
    win = _permute_win(w_in, math.gcd(w_in.shape[1], 256))

    wq = w_uq.reshape(depth, MLA_Q_RANK, MLA_HEADS, MLA_NOPE + MLA_ROPE)
    wqn = wq[..., :MLA_NOPE].reshape(depth, MLA_Q_RANK, MLA_HEADS * MLA_NOPE).astype(BF16)
    wr = wq[..., MLA_NOPE:]
    wqr = jnp.concatenate([wr, swap(wr)], axis=-1).reshape(depth, MLA_Q_RANK, MLA_HEADS * LANES).astype(BF16)

    wkv = w_ukv.reshape(depth, MLA_KV_RANK, MLA_HEADS, MLA_NOPE + MLA_V)
    wk = wkv[..., :MLA_NOPE].reshape(depth, MLA_KV_RANK, MLA_HEADS * MLA_NOPE).astype(BF16)
    wv = wkv[..., MLA_NOPE:].reshape(depth, MLA_KV_RANK, MLA_HEADS * MLA_V).astype(BF16)

    wg = jnp.zeros((depth, LANES, 2 * GLA_QK), F32)
    wg = wg.at[:, _LR_LANE:_LR_LANE + GLA_GATE_RANK, :GLA_QK].set(wgf)
    wg = wg.at[:, _LR_LANE + GLA_GATE_RANK:_LR_LANE + 2 * GLA_GATE_RANK, GLA_QK:].set(wgb)
    bg = jnp.concatenate([bgf, bgb], axis=-1).reshape(depth, 1, 2 * GLA_QK)
    wq = jnp.concatenate([wqn, wqr], axis=-1)
    wkv = jnp.concatenate([wk, wv], axis=-1)
    return win, wq, wkv, wg.astype(BF16), bg


def _layer_spec(a, layer):
    return pl.BlockSpec((None,) + a.shape[1:], lambda i: (layer, 0, 0))


def _front(h, ln, weights, qn, kvn, tabs, layer, tm, chunk):
    tokens, d = h.shape
    assert tm % chunk == 0, (tm, chunk)
    win, wq, wkv, wg, bg = weights
    tq, tkc, tks = tabs
    row = lambda n: pl.BlockSpec((tm, n), lambda i: (i, 0))
    full = lambda a: _layer_spec(a, layer)
    ln, qn, kvn = (a.reshape(a.shape[0], 1, a.shape[1]) for a in (ln, qn, kvn))
    outs = [
        (MLA_HEADS * QK_PAD, BF16),
        (MLA_HEADS * QK_PAD, BF16),
        (MLA_WIDTH, BF16),
        (MLA_WIDTH, BF16),
        (2 * GLA_QK, BF16),
        (2 * GLA_QK, BF16),
        None,
        (GLA_WIDTH, BF16),
        (GLA_WIDTH, BF16),
    ]
    dec_spec = pl.BlockSpec((tm // chunk, 1, GLA_DEC), lambda i: (i, 0, 0))
    dec_shape = jax.ShapeDtypeStruct((tokens // chunk, 1, GLA_DEC), F32)
    return pl.pallas_call(
        functools.partial(_front_kernel, chunk=chunk),
        grid=(tokens // tm,),
        in_specs=[row(d), full(ln), full(win), full(qn), full(wq), full(kvn),
                  full(wkv), full(wg), full(bg), row(LANES), row(LANES), row(LANES)],
        out_specs=[dec_spec if o is None else row(o[0]) for o in outs],
        out_shape=[dec_shape if o is None else jax.ShapeDtypeStruct((tokens, o[0]), o[1]) for o in outs],
        compiler_params=pltpu.CompilerParams(dimension_semantics=("arbitrary",), vmem_limit_bytes=VMEM_LIMIT),
        name="front",
    )(h, ln, win, qn, wq, kvn, wkv, wg, bg, tq, tkc, tks)


def _attn_kernel(q_ref, k_ref, v_ref, g_ref, o_ref, s_ref, m_ref, vext_ref, *, tq):
    vext_ref[:, :MLA_V] = v_ref[0]
    vext_ref[:, MLA_V:] = jnp.ones((v_ref.shape[1], MLA_V), BF16)
    groups = k_ref.shape[1] // LANES
    once = jnp.minimum(pl.program_id(0) + 1, 1)

    @pl.loop(0, q_ref.shape[1] // tq)
    def _(qi):
        rows = pl.ds(pl.multiple_of(qi * tq, tq), tq)

        @pl.loop(0, once)
        def _(_):
            s = _dot_nt(q_ref[0, rows, :], k_ref[0])
            s_ref[...] = s
            m_acc = s[:, :LANES]
            for c in range(1, groups):
                m_acc = jnp.maximum(m_acc, s[:, c * LANES:(c + 1) * LANES])
            m_ref[...] = jnp.broadcast_to(jnp.max(m_acc, axis=-1, keepdims=True), (tq, LANES))

        m_b = m_ref[...]

        @pl.loop(0, once)
        def _(_):
            cols = [jnp.exp2((s_ref[:, c * LANES:(c + 1) * LANES] - m_b).astype(BF16)) for c in range(groups)]
            acc = _dot(jnp.concatenate(cols, axis=1), vext_ref[...])
            o_ref[0, rows, :] = (acc[:, :MLA_V] / acc[:, MLA_V:] * g_ref[0, rows, :].astype(F32)).astype(o_ref.dtype)


def _attention(q, k, v, sgate, tq):
    b, s, _ = q.shape
    head = lambda w: pl.BlockSpec((1, s, w), lambda bi, hi: (bi, 0, hi))
    return pl.pallas_call(
        functools.partial(_attn_kernel, tq=tq),
        grid=(b, MLA_HEADS),
        in_specs=[head(QK_PAD), head(QK_PAD), head(MLA_V), head(MLA_V)],
        out_specs=head(MLA_V),
        out_shape=jax.ShapeDtypeStruct((b, s, MLA_WIDTH), BF16),
        scratch_shapes=[pltpu.VMEM((tq, s), F32), pltpu.VMEM((tq, LANES), F32), pltpu.VMEM((s, 2 * MLA_V), BF16)],
        compiler_params=pltpu.CompilerParams(
            dimension_semantics=("arbitrary", "arbitrary"), vmem_limit_bytes=VMEM_LIMIT),
        name="mla_attention",
    )(q, k, v, sgate)


def _gla_direction(g_ref, v_ref, dec_ref, state_ref, forward, chunk, emit):
    c = chunk
    n_sub = g_ref.shape[1] // c
    row = lax.broadcasted_iota(jnp.int32, (c, c), 0)
    col = lax.broadcasted_iota(jnp.int32, (c, c), 1)
    tri = (col <= row) if forward else (col >= row)
    lane_head = lax.broadcasted_iota(jnp.int32, (c, LANES), 1) // GLA_DK
    zero = jnp.zeros((c, LANES), BF16)
    base = 0 if forward else 3 * GLA_QK
    state_ref[...] = jnp.zeros_like(state_ref)
    for sub in (range(n_sub) if forward else reversed(range(n_sub))):
        r = slice(sub * c, (sub + 1) * c)
        for hd in range(GLA_HEADS):
            pair = (hd // 2) * LANES
            fac = lambda i: dec_ref[sub, :, base + i * GLA_QK + pair:base + i * GLA_QK + pair + LANES]
            q_in = g_ref[0, r, pair:pair + LANES]
            k_in = jnp.where(lane_head == (hd % 2), g_ref[0, r, GLA_QK + pair:GLA_QK + pair + LANES], zero)
            vh = v_ref[0, r, hd * GLA_DV:(hd + 1) * GLA_DV]
            a = jnp.where(tri, _dot_nt(q_in, k_in), 0.0).astype(BF16)
            st = state_ref[hd]
            emit(r, hd, _dot(a, vh) + _dot_nt(q_in, (st * fac(1)).astype(BF16)))
            state_ref[hd] = st * fac(0) + _dot_tn(vh, k_in) * fac(2)


def _gla_kernel(gf_ref, gb_ref, v_ref, dec_ref, sgg_ref, on_ref, y_ref, state_ref, *, chunk):
    cols = lambda hd: slice(hd * GLA_DV, (hd + 1) * GLA_DV)

    def keep_forward(r, hd, o):
        y_ref[0, r, cols(hd)] = o.astype(y_ref.dtype)

    def combine(r, hd, o):
        og = y_ref[0, r, cols(hd)].astype(F32) + o
        y_ref[0, r, cols(hd)] = (_rms(og, on_ref[...]) * sgg_ref[0, r, cols(hd)].astype(F32)).astype(y_ref.dtype)

    _gla_direction(gf_ref, v_ref, dec_ref, state_ref, True, chunk, keep_forward)
    _gla_direction(gb_ref, v_ref, dec_ref, state_ref, False, chunk, combine)


def _gla(gf, gb, gv, dec, sgg, out_norm, layer, chunk):
    b, s, _ = gv.shape
    n = s // chunk
    seq = lambda w: pl.BlockSpec((1, s, w), lambda bi: (bi, 0, 0))
    on = out_norm.reshape(out_norm.shape[0], 1, out_norm.shape[1])
    return pl.pallas_call(
        functools.partial(_gla_kernel, chunk=chunk),
        grid=(b,),
        in_specs=[seq(2 * GLA_QK), seq(2 * GLA_QK), seq(GLA_WIDTH),
                  pl.BlockSpec((n, 1, GLA_DEC), lambda bi: (bi, 0, 0)), seq(GLA_WIDTH), _layer_spec(on, layer)],
        out_specs=seq(GLA_WIDTH),
        out_shape=jax.ShapeDtypeStruct((b, s, GLA_WIDTH), BF16),
        scratch_shapes=[pltpu.VMEM((GLA_HEADS, GLA_DV, LANES), F32)],
        compiler_params=pltpu.CompilerParams(dimension_semantics=("arbitrary",), vmem_limit_bytes=VMEM_LIMIT),
        name="gla_scan",
    )(gf, gb, gv, dec, sgg, on)


def _back_kernel(h_ref, ym_ref, yg_ref, wout_ref, pn_ref, wpg_ref, p_ref, wpp_ref, fn_ref, o_ref, *, last):
    y = jnp.concatenate([ym_ref[...], yg_ref[...]], axis=1)
    h1 = h_ref[...] + _dot(y, wout_ref[...])
    gate = jax.nn.sigmoid(_dot(_rms(h1, pn_ref[...]).astype(BF16), wpg_ref[...]))
    h2 = h1 + gate * _dot(p_ref[0].astype(BF16), wpp_ref[...])
    o_ref[...] = _rms(h2, fn_ref[...]) if last else h2


def _back(h, y_mla, y_gla, w_out, ple_norm, w_pg, p, layer, w_pp, final_norm, tm, last):
    tokens, d = h.shape
    row = lambda n: pl.BlockSpec((tm, n), lambda i: (i, 0))
    full = lambda a: _layer_spec(a, layer)
    p_spec = pl.BlockSpec((1, tm, p.shape[2]), lambda i: (layer, i, 0))
    pn = ple_norm.reshape(ple_norm.shape[0], 1, ple_norm.shape[1])
    fn = final_norm.reshape(1, -1)
    return pl.pallas_call(
        functools.partial(_back_kernel, last=last),
        grid=(tokens // tm,),
        in_specs=[row(d), row(MLA_WIDTH), row(GLA_WIDTH), full(w_out), full(pn), full(w_pg), p_spec, full(w_pp),
                  pl.BlockSpec(fn.shape, lambda i: (0, 0))],
        out_specs=row(d),
        out_shape=jax.ShapeDtypeStruct((tokens, d), F32),
        compiler_params=pltpu.CompilerParams(dimension_semantics=("arbitrary",), vmem_limit_bytes=VMEM_LIMIT),
        name="back",
    )(h, y_mla, y_gla, w_out, pn, w_pg, p, w_pp, fn)


def kernel(x, p, positions, ln_mix, w_in, mla_q_norm, w_uq, mla_kv_norm, w_ukv, gla_w_gate_fwd, gla_b_gate_fwd,
           gla_w_gate_bwd, gla_b_gate_bwd, gla_out_norm, w_out, ple_norm, w_ple_gate, w_ple_proj, final_norm):
    batch, seq, d = x.shape
    depth = w_in.shape[0]
    tokens = batch * seq
    t = _tiles(batch, seq)
    tabs = _rope_tables(positions, t["rope_rows"])
    h = x.reshape(tokens, d)
    p_all = p.reshape(depth, tokens, p.shape[-1])
    seq3 = lambda a: a.reshape(batch, seq, a.shape[-1])
    weights = _front_weights(w_in, w_uq, w_ukv, gla_w_gate_fwd, gla_b_gate_fwd, gla_w_gate_bwd, gla_b_gate_bwd)
    w_out, w_ple_gate, w_ple_proj = (w.astype(BF16) for w in (w_out, w_ple_gate, w_ple_proj))
    for i in range(depth):
        q, k, v, sga, gf, gb, dec, gv, sgg = _front(
            h, ln_mix, weights, mla_q_norm, mla_kv_norm, tabs, i, t["row_tile"], t["gla_chunk"])
        y_mla = _attention(seq3(q), seq3(k), seq3(v), seq3(sga), t["q_tile"])
        y_gla = _gla(seq3(gf), seq3(gb), seq3(gv), dec, seq3(sgg), gla_out_norm, i, t["gla_chunk"])
        h = _back(h, y_mla.reshape(tokens, -1), y_gla.reshape(tokens, -1), w_out, ple_norm, w_ple_gate, p_all, i,
                  w_ple_proj, final_norm, t["back_tile"], last=(i == depth - 1))
    return h.reshape(batch, seq, d)
```

```python
import functools
import math

import jax
import jax.numpy as jnp
from jax import lax
from jax.experimental import pallas as pl
from jax.experimental.pallas import tpu as pltpu

EPS = 1e-6
MLA_HEADS = 4
MLA_Q_RANK = 384
MLA_KV_RANK = 256
MLA_NOPE = 128
MLA_ROPE = 64
MLA_V = 128
MLA_WIDTH = MLA_HEADS * MLA_V
ROPE_THETA = 10000.0
GLA_HEADS = 4
GLA_DK = 64
GLA_DV = 128
GLA_WIDTH = GLA_HEADS * GLA_DV
GLA_QK = GLA_HEADS * GLA_DK
GLA_GATE_RANK = 16
GLA_TAU = 16.0
GLA_DEC = 2 * 3 * GLA_QK

LANES = 128
QK_PAD = 2 * LANES
VMEM_LIMIT = 48 * 1024 * 1024
VMEM_LIMIT_SCAN = 52 * 1024 * 1024

Q_SCALE = (MLA_NOPE + MLA_ROPE) ** -0.5 * math.log2(math.e)

F32 = jnp.float32
BF16 = jnp.bfloat16


def _tiles(batch, seq):
    tokens = batch * seq
    row_tile = math.gcd(tokens, 512)
    return dict(
        row_tile=row_tile,
        back_tile=math.gcd(tokens, 1024),
        q_tile=math.gcd(seq, 1024),
        gla_chunk=math.gcd(seq, 64),
        rope_rows=math.gcd(tokens // 4, 1024),
    )


def _dot(a, b):
    return jnp.dot(a, b, preferred_element_type=F32)


def _dot_nt(a, b):
    return lax.dot_general(a, b, (((1,), (1,)), ((), ())), preferred_element_type=F32)


def _dot_tn(a, b):
    return lax.dot_general(a, b, (((0,), (0,)), ((), ())), preferred_element_type=F32)


def _rms(x, g):
    return x * lax.rsqrt(jnp.mean(x * x, axis=-1, keepdims=True) + EPS) * g


def _silu(x):
    return x * jax.nn.sigmoid(x)


def _log2_sigmoid(x):
    y = x * math.log2(math.e)
    return jnp.minimum(y, 0.0) - jnp.log2(1.0 + jnp.exp2(-jnp.abs(y)))


def _split2(x):
    hi = x.astype(BF16)
    lo = (x - hi.astype(F32)).astype(BF16)
    return hi, lo


def _rope_table_kernel(pos_ref, inv_ref, tq_ref, tkc_ref, tks_ref):
    half = MLA_ROPE // 2
    ang = pos_ref[...] * inv_ref[...]
    cos = jnp.cos(ang)
    sin = jnp.sin(ang)
    lane = lax.broadcasted_iota(jnp.int32, cos.shape, 1)
    for i in range(LANES // half):
        place = lambda x, dst: x if dst == i else pltpu.roll(x, ((dst - i) * half) % LANES, axis=1)
        base = jnp.where(lane < half, place(cos, 0),
                         jnp.where(lane < 2 * half, place(cos, 1),
                                   jnp.where(lane < 3 * half, -place(sin, 2), place(sin, 3))))
        tq_ref[i] = base * Q_SCALE
        tkc_ref[i] = jnp.where(lane < 2 * half, base, 0.0)
        tks_ref[i] = jnp.where(lane < 2 * half, pltpu.roll(base, 2 * half, axis=1), 0.0)


def _rope_tables(positions, rows):
    half = MLA_ROPE // 2
    per_row = LANES // half
    tokens = positions.size
    n = tokens // per_row
    inv = ROPE_THETA ** (-jnp.arange(half, dtype=F32) / half)
    pos = jnp.broadcast_to(positions.reshape(per_row, n, 1).astype(F32), (per_row, n, half))
    pos = pos.transpose(1, 0, 2).reshape(n, LANES)
    inv4 = jnp.tile(inv, per_row).reshape(1, LANES)
    out_spec = pl.BlockSpec((per_row, rows, LANES), lambda i: (0, i, 0))
    tabs = pl.pallas_call(
        _rope_table_kernel,
        grid=(n // rows,),
        in_specs=[pl.BlockSpec((rows, LANES), lambda i: (i, 0)), pl.BlockSpec((1, LANES), lambda i: (0, 0))],
        out_specs=[out_spec] * 3,
        out_shape=[jax.ShapeDtypeStruct((per_row, n, LANES), F32)] * 3,
        name="rope_tables",
    )(pos, inv4)
    return tuple(t.reshape(tokens, LANES) for t in tabs)


_C_CQ = 0
_C_KRLR = _C_CQ + MLA_Q_RANK
_C_CKV = _C_KRLR + LANES
_C_GA = _C_CKV + MLA_KV_RANK
_C_GQ = _C_GA + MLA_WIDTH
_C_GK = _C_GQ + GLA_QK
_C_GV = _C_GK + GLA_QK
_C_GG = _C_GV + GLA_WIDTH
_C_END = _C_GG + GLA_WIDTH
_LR_LANE = MLA_ROPE
_W_CQ = 0
_W_CKV = _W_CQ + MLA_Q_RANK
_W_KR = _W_CKV + MLA_KV_RANK
_W_GA = _W_KR + MLA_ROPE
_W_LR = _W_GA + MLA_WIDTH + 2 * GLA_QK + GLA_WIDTH
_W_GG = _W_LR + 2 * GLA_GATE_RANK
_W_END = _W_GG + GLA_WIDTH


def _win_kernel(wt_ref, o_ref):
    take = lambda a, b: wt_ref[a:b, :]
    pad = jnp.zeros((LANES - MLA_ROPE - 2 * GLA_GATE_RANK, wt_ref.shape[1]), wt_ref.dtype)
    wt = jnp.concatenate([take(_W_CQ, _W_CKV), take(_W_KR, _W_GA), take(_W_LR, _W_GG), pad,
                          take(_W_CKV, _W_KR), take(_W_GA, _W_LR), take(_W_GG, _W_END)], axis=0)
    o_ref[...] = wt.T.astype(BF16)


def _permute_win(w_in, rows):
    depth, d, n = w_in.shape
    return pl.pallas_call(
        _win_kernel,
        grid=(depth, d // rows),
        in_specs=[pl.BlockSpec((None, n, rows), lambda l, i: (l, 0, i))],
        out_specs=pl.BlockSpec((None, rows, _C_END), lambda l, i: (l, i, 0)),
        out_shape=jax.ShapeDtypeStruct((depth, d, _C_END), BF16),
        name="permute_w_in",
    )(jnp.swapaxes(w_in, 1, 2))


def _front_kernel(h_ref, ln_ref, win_ref, qn_ref, wq_ref, kvn_ref, wkv_ref,
                  wg_ref, bg_ref, tq_ref, tkc_ref, tks_ref,
                  q_ref, k_ref, v_ref, sga_ref, gf_ref, gb_ref, dec_ref, gv_ref, sgg_ref, *, chunk):
    xn = _rms(h_ref[...], ln_ref[...]).astype(BF16)
    u = _dot(xn, win_ref[...])

    def proj(c0, c1):
        return u[:, c0:c1]

    cq_krlr = proj(_C_CQ, _C_CKV)
    krlr = cq_krlr[:, _C_KRLR:_C_CKV]

    cqn = _rms(cq_krlr[:, :_C_KRLR], qn_ref[...]).astype(BF16)
    q_all = _dot(cqn, wq_ref[...])
    q_nope = q_all[:, :MLA_HEADS * MLA_NOPE] * Q_SCALE
    q_rope = q_all[:, MLA_HEADS * MLA_NOPE:]
    tq = tq_ref[...]
    for hd in range(MLA_HEADS):
        g = slice(hd * LANES, (hd + 1) * LANES)
        q_ref[:, hd * QK_PAD:hd * QK_PAD + LANES] = q_nope[:, g].astype(BF16)
        qr = q_rope[:, g] * tq
        q_ref[:, hd * QK_PAD + LANES:(hd + 1) * QK_PAD] = (qr + pltpu.roll(qr, MLA_ROPE, axis=1)).astype(BF16)

    ckvn = _rms(proj(_C_CKV, _C_GA), kvn_ref[...]).astype(BF16)
    kv = _dot(ckvn, wkv_ref[...])
    k_nope = kv[:, :MLA_HEADS * MLA_NOPE]
    v_ref[...] = kv[:, MLA_HEADS * MLA_NOPE:].astype(BF16)
    half = MLA_ROPE // 2
    lane = lax.broadcasted_iota(jnp.int32, krlr.shape, 1)
    kr_sw = jnp.where(lane < half, pltpu.roll(krlr, LANES - half, axis=1), pltpu.roll(krlr, half, axis=1))
    kr = (krlr * tkc_ref[...] + kr_sw * tks_ref[...]).astype(BF16)
    for hd in range(MLA_HEADS):
        k_ref[:, hd * QK_PAD:hd * QK_PAD + LANES] = k_nope[:, hd * LANES:(hd + 1) * LANES].astype(BF16)
        k_ref[:, hd * QK_PAD + LANES:(hd + 1) * QK_PAD] = kr

    sga_ref[...] = _silu(proj(_C_GA, _C_GQ)).astype(BF16)
    sgg_ref[...] = _silu(proj(_C_GG, _C_END)).astype(BF16)

    gq = proj(_C_GQ, _C_GK) * GLA_DK ** -0.5
    gk = proj(_C_GK, _C_GV)
    gv_ref[...] = proj(_C_GV, _C_GG).astype(BF16)
    la = _log2_sigmoid(_dot(krlr.astype(BF16), wg_ref[...]) + bg_ref[...])
    row = lax.broadcasted_iota(jnp.int32, (chunk, chunk), 0)
    col = lax.broadcasted_iota(jnp.int32, (chunk, chunk), 1)
    for d, out_ref in enumerate((gf_ref, gb_ref)):
        forward = d == 0
        tri = jnp.where((col <= row) if forward else (col >= row), 1.0 / GLA_TAU, 0.0).astype(BF16)
        for ci in range(la.shape[0] // chunk):
            r = slice(ci * chunk, (ci + 1) * chunk)
            hi, lo = _split2(la[r, d * GLA_QK:(d + 1) * GLA_QK])
            b = _dot(tri, jnp.concatenate([hi, lo], axis=1))
            b = b[:, :GLA_QK] + b[:, GLA_QK:]
            mid = chunk // 2 - 1 if forward else chunk // 2
            end = chunk - 1 if forward else 0
            b_mid = b[mid:mid + 1, :]
            b_end = b[end:end + 1, :]
            out_ref[r, :GLA_QK] = (gq[r] * jnp.exp2(b - b_mid)).astype(BF16)
            out_ref[r, GLA_QK:] = (gk[r] * jnp.exp2(b_mid - b)).astype(BF16)
            base = d * 3 * GLA_QK
            dec_ref[ci:ci + 1, base:base + GLA_QK] = jnp.exp2(b_end)
            dec_ref[ci:ci + 1, base + GLA_QK:base + 2 * GLA_QK] = jnp.exp2(b_mid)
            dec_ref[ci:ci + 1, base + 2 * GLA_QK:base + 3 * GLA_QK] = jnp.exp2(b_end - b_mid)


def _front_weights(w_in, w_uq, w_ukv, wgf, bgf, wgb, bgb):
    depth = w_in.shape[0]
    half = MLA_ROPE // 2
    swap = lambda w: jnp.concatenate([w[..., half:], w[..., :half]], axis=-1)
    win = _permute_win(w_in, math.gcd(w_in.shape[1], 256))

    wq = w_uq.reshape(depth, MLA_Q_RANK, MLA_HEADS, MLA_NOPE + MLA_ROPE)
    wqn = wq[..., :MLA_NOPE].reshape(depth, MLA_Q_RANK, MLA_HEADS * MLA_NOPE).astype(BF16)
    wr = wq[..., MLA_NOPE:]
    wqr = jnp.concatenate([wr, swap(wr)], axis=-1).reshape(depth, MLA_Q_RANK, MLA_HEADS * LANES).astype(BF16)

    wkv = w_ukv.reshape(depth, MLA_KV_RANK, MLA_HEADS, MLA_NOPE + MLA_V)
    wk = wkv[..., :MLA_NOPE].reshape(depth, MLA_KV_RANK, MLA_HEADS * MLA_NOPE).astype(BF16)
    wv = wkv[..., MLA_NOPE:].reshape(depth, MLA_KV_RANK, MLA_HEADS * MLA_V).astype(BF16)

    wg = jnp.zeros((depth, LANES, 2 * GLA_QK), F32)
    wg = wg.at[:, _LR_LANE:_LR_LANE + GLA_GATE_RANK, :GLA_QK].set(wgf)
    wg = wg.at[:, _LR_LANE + GLA_GATE_RANK:_LR_LANE + 2 * GLA_GATE_RANK, GLA_QK:].set(wgb)
    bg = jnp.concatenate([bgf, bgb], axis=-1).reshape(depth, 1, 2 * GLA_QK)
    wq = jnp.concatenate([wqn, wqr], axis=-1)
    wkv = jnp.concatenate([wk, wv], axis=-1)
    return win, wq, wkv, wg.astype(BF16), bg


def _layer_spec(a, layer):
    return pl.BlockSpec((None,) + a.shape[1:], lambda i: (layer, 0, 0))


def _front(h, ln, weights, qn, kvn, tabs, layer, tm, chunk):
    tokens, d = h.shape
    assert tm % chunk == 0, (tm, chunk)
    win, wq, wkv, wg, bg = weights
    tq, tkc, tks = tabs
    row = lambda n: pl.BlockSpec((tm, n), lambda i: (i, 0))
    full = lambda a: _layer_spec(a, layer)
    ln, qn, kvn = (a.reshape(a.shape[0], 1, a.shape[1]) for a in (ln, qn, kvn))
    outs = [
        (MLA_HEADS * QK_PAD, BF16),
        (MLA_HEADS * QK_PAD, BF16),
        (MLA_WIDTH, BF16),
        (MLA_WIDTH, BF16),
        (2 * GLA_QK, BF16),
        (2 * GLA_QK, BF16),
        None,
        (GLA_WIDTH, BF16),
        (GLA_WIDTH, BF16),
    ]
    dec_spec = pl.BlockSpec((tm // chunk, GLA_DEC), lambda i: (i, 0))
    dec_shape = jax.ShapeDtypeStruct((tokens // chunk, GLA_DEC), F32)
    return pl.pallas_call(
        functools.partial(_front_kernel, chunk=chunk),
        grid=(tokens // tm,),
        in_specs=[row(d), full(ln), full(win), full(qn), full(wq), full(kvn),
                  full(wkv), full(wg), full(bg), row(LANES), row(LANES), row(LANES)],
        out_specs=[dec_spec if o is None else row(o[0]) for o in outs],
        out_shape=[dec_shape if o is None else jax.ShapeDtypeStruct((tokens, o[0]), o[1]) for o in outs],
        compiler_params=pltpu.CompilerParams(dimension_semantics=("arbitrary",), vmem_limit_bytes=VMEM_LIMIT),
        name="front",
    )(h, ln, win, qn, wq, kvn, wkv, wg, bg, tq, tkc, tks)


def _attn_kernel(q_ref, k_ref, v_ref, g_ref, o_ref, s_ref, m_ref, vext_ref, *, tq):
    vext_ref[:, :MLA_V] = v_ref[0]
    vext_ref[:, MLA_V:] = jnp.ones((v_ref.shape[1], MLA_V), BF16)
    groups = k_ref.shape[1] // LANES
    once = jnp.minimum(pl.program_id(0) + 1, 1)

    @pl.loop(0, q_ref.shape[1] // tq)
    def _(qi):
        rows = pl.ds(pl.multiple_of(qi * tq, tq), tq)

        @pl.loop(0, once)
        def _(_):
            s = _dot_nt(q_ref[0, rows, :], k_ref[0])
            s_ref[...] = s
            m_acc = s[:, :LANES]
            for c in range(1, groups):
                m_acc = jnp.maximum(m_acc, s[:, c * LANES:(c + 1) * LANES])
            m_ref[...] = jnp.broadcast_to(jnp.max(m_acc, axis=-1, keepdims=True), (tq, LANES))

        m_b = m_ref[...]

        @pl.loop(0, once)
        def _(_):
            cols = [jnp.exp2((s_ref[:, c * LANES:(c + 1) * LANES] - m_b).astype(BF16)) for c in range(groups)]
            acc = _dot(jnp.concatenate(cols, axis=1), vext_ref[...])
            o_ref[0, rows, :] = (acc[:, :MLA_V] / acc[:, MLA_V:] * g_ref[0, rows, :].astype(F32)).astype(o_ref.dtype)


def _attention(q, k, v, sgate, tq):
    b, s, _ = q.shape
    head = lambda w: pl.BlockSpec((1, s, w), lambda bi, hi: (bi, 0, hi))
    return pl.pallas_call(
        functools.partial(_attn_kernel, tq=tq),
        grid=(b, MLA_HEADS),
        in_specs=[head(QK_PAD), head(QK_PAD), head(MLA_V), head(MLA_V)],
        out_specs=head(MLA_V),
        out_shape=jax.ShapeDtypeStruct((b, s, MLA_WIDTH), BF16),
        scratch_shapes=[pltpu.VMEM((tq, s), F32), pltpu.VMEM((tq, LANES), F32), pltpu.VMEM((s, 2 * MLA_V), BF16)],
        compiler_params=pltpu.CompilerParams(
            dimension_semantics=("arbitrary", "arbitrary"), vmem_limit_bytes=VMEM_LIMIT),
        name="mla_attention",
    )(q, k, v, sgate)


def _gla_direction(g_ref, v_ref, dec_ref, state_ref, forward, chunk, emit):
    c = chunk
    n_sub = g_ref.shape[1] // c
    row = lax.broadcasted_iota(jnp.int32, (c, c), 0)
    col = lax.broadcasted_iota(jnp.int32, (c, c), 1)
    tri = (col <= row) if forward else (col >= row)
    lane_head = lax.broadcasted_iota(jnp.int32, (c, LANES), 1) // GLA_DK
    zero = jnp.zeros((c, LANES), BF16)
    base = 0 if forward else 3 * GLA_QK
    state_ref[...] = jnp.zeros_like(state_ref)
    for sub in (range(n_sub) if forward else reversed(range(n_sub))):
        r = slice(sub * c, (sub + 1) * c)
        for hd in range(GLA_HEADS):
            pair = (hd // 2) * LANES
            fac = lambda i: dec_ref[sub:sub + 1, base + i * GLA_QK + pair:base + i * GLA_QK + pair + LANES]
            q_in = g_ref[0, r, pair:pair + LANES]
            k_in = jnp.where(lane_head == (hd % 2), g_ref[0, r, GLA_QK + pair:GLA_QK + pair + LANES], zero)
            vh = v_ref[0, r, hd * GLA_DV:(hd + 1) * GLA_DV]
            a = jnp.where(tri, _dot_nt(q_in, k_in), 0.0).astype(BF16)
            st = state_ref[hd]
            emit(r, hd, _dot(a, vh) + _dot_nt(q_in, (st * fac(1)).astype(BF16)))
            state_ref[hd] = st * fac(0) + _dot_tn(vh, k_in) * fac(2)


def _gla_kernel(gf_ref, gb_ref, v_ref, dec_ref, sgg_ref, on_ref, y_ref, state_ref, *, chunk):
    cols = lambda hd: slice(hd * GLA_DV, (hd + 1) * GLA_DV)

    def keep_forward(r, hd, o):
        y_ref[0, r, cols(hd)] = o.astype(y_ref.dtype)

    def combine(r, hd, o):
        og = y_ref[0, r, cols(hd)].astype(F32) + o
        y_ref[0, r, cols(hd)] = (_rms(og, on_ref[...]) * sgg_ref[0, r, cols(hd)].astype(F32)).astype(y_ref.dtype)

    _gla_direction(gf_ref, v_ref, dec_ref, state_ref, True, chunk, keep_forward)
    _gla_direction(gb_ref, v_ref, dec_ref, state_ref, False, chunk, combine)


def _gla(gf, gb, gv, dec, sgg, out_norm, layer, chunk):
    b, s, _ = gv.shape
    n = s // chunk
    seq = lambda w: pl.BlockSpec((1, s, w), lambda bi: (bi, 0, 0))
    on = out_norm.reshape(out_norm.shape[0], 1, out_norm.shape[1])
    return pl.pallas_call(
        functools.partial(_gla_kernel, chunk=chunk),
        grid=(b,),
        in_specs=[seq(2 * GLA_QK), seq(2 * GLA_QK), seq(GLA_WIDTH),
                  pl.BlockSpec((n, GLA_DEC), lambda bi: (bi, 0)), seq(GLA_WIDTH), _layer_spec(on, layer)],
        out_specs=seq(GLA_WIDTH),
        out_shape=jax.ShapeDtypeStruct((b, s, GLA_WIDTH), BF16),
        scratch_shapes=[pltpu.VMEM((GLA_HEADS, GLA_DV, LANES), F32)],
        compiler_params=pltpu.CompilerParams(dimension_semantics=("arbitrary",), vmem_limit_bytes=VMEM_LIMIT_SCAN),
        name="gla_scan",
    )(gf, gb, gv, dec, sgg, on)


def _back_kernel(h_ref, ym_ref, yg_ref, wout_ref, pn_ref, wpg_ref, p_ref, wpp_ref, fn_ref, o_ref, *, last):
    y = jnp.concatenate([ym_ref[...], yg_ref[...]], axis=1)
    h1 = h_ref[...] + _dot(y, wout_ref[...])
    gate = jax.nn.sigmoid(_dot(_rms(h1, pn_ref[...]).astype(BF16), wpg_ref[...]))
    h2 = h1 + gate * _dot(p_ref[0].astype(BF16), wpp_ref[...])
    o_ref[...] = _rms(h2, fn_ref[...]) if last else h2


def _back(h, y_mla, y_gla, w_out, ple_norm, w_pg, p, layer, w_pp, final_norm, tm, last):
    tokens, d = h.shape
    row = lambda n: pl.BlockSpec((tm, n), lambda i: (i, 0))
    full = lambda a: _layer_spec(a, layer)
    p_spec = pl.BlockSpec((1, tm, p.shape[2]), lambda i: (layer, i, 0))
    pn = ple_norm.reshape(ple_norm.shape[0], 1, ple_norm.shape[1])
    fn = final_norm.reshape(1, -1)
    return pl.pallas_call(
        functools.partial(_back_kernel, last=last),
        grid=(tokens // tm,),
        in_specs=[row(d), row(MLA_WIDTH), row(GLA_WIDTH), full(w_out), full(pn), full(w_pg), p_spec, full(w_pp),
                  pl.BlockSpec(fn.shape, lambda i: (0, 0))],
        out_specs=row(d),
        out_shape=jax.ShapeDtypeStruct((tokens, d), F32),
        compiler_params=pltpu.CompilerParams(dimension_semantics=("arbitrary",), vmem_limit_bytes=VMEM_LIMIT),
        name="back",
    )(h, y_mla, y_gla, w_out, pn, w_pg, p, w_pp, fn)


def kernel(x, p, positions, ln_mix, w_in, mla_q_norm, w_uq, mla_kv_norm, w_ukv, gla_w_gate_fwd, gla_b_gate_fwd,
           gla_w_gate_bwd, gla_b_gate_bwd, gla_out_norm, w_out, ple_norm, w_ple_gate, w_ple_proj, final_norm):
    batch, seq, d = x.shape
    depth = w_in.shape[0]
    tokens = batch * seq
    t = _tiles(batch, seq)
    tabs = _rope_tables(positions, t["rope_rows"])
    h = x.reshape(tokens, d)
    p_all = p.reshape(depth, tokens, p.shape[-1])
    seq3 = lambda a: a.reshape(batch, seq, a.shape[-1])
    weights = _front_weights(w_in, w_uq, w_ukv, gla_w_gate_fwd, gla_b_gate_fwd, gla_w_gate_bwd, gla_b_gate_bwd)
    w_out, w_ple_gate, w_ple_proj = (w.astype(BF16) for w in (w_out, w_ple_gate, w_ple_proj))
    for i in range(depth):
        q, k, v, sga, gf, gb, dec, gv, sgg = _front(
            h, ln_mix, weights, mla_q_norm, mla_kv_norm, tabs, i, t["row_tile"], t["gla_chunk"])
        y_mla = _attention(seq3(q), seq3(k), seq3(v), seq3(sga), t["q_tile"])
        y_gla = _gla(seq3(gf), seq3(gb), seq3(gv), dec, seq3(sgg), gla_out_norm, i, t["gla_chunk"])
        h = _back(h, y_mla.reshape(tokens, -1), y_gla.reshape(tokens, -1), w_out, ple_norm, w_ple_gate, p_all, i,
                  w_ple_proj, final_norm, t["back_tile"], last=(i == depth - 1))
    return h.reshape(batch, seq, d)
```

```python
import functools
import math

import jax
import jax.numpy as jnp
from jax import lax
from jax.experimental import pallas as pl
from jax.experimental.pallas import tpu as pltpu

EPS = 1e-6
MLA_HEADS = 4
MLA_Q_RANK = 384
MLA_KV_RANK = 256
MLA_NOPE = 128
MLA_ROPE = 64
MLA_V = 128
MLA_WIDTH = MLA_HEADS * MLA_V
ROPE_THETA = 10000.0
GLA_HEADS = 4
GLA_DK = 64
GLA_DV = 128
GLA_WIDTH = GLA_HEADS * GLA_DV
GLA_QK = GLA_HEADS * GLA_DK
GLA_GATE_RANK = 16
GLA_TAU = 16.0
GLA_DEC = 2 * 3 * GLA_QK

LANES = 128
QK_PAD = 2 * LANES
VMEM_LIMIT = 48 * 1024 * 1024
VMEM_LIMIT_SCAN = 52 * 1024 * 1024

Q_SCALE = (MLA_NOPE + MLA_ROPE) ** -0.5 * math.log2(math.e)

F32 = jnp.float32
BF16 = jnp.bfloat16


def _tiles(batch, seq):
    tokens = batch * seq
    row_tile = math.gcd(tokens, 512)
    return dict(
        row_tile=row_tile,
        back_tile=math.gcd(tokens, 1024),
        q_tile=math.gcd(seq, 1024),
        gla_chunk=math.gcd(seq, 64),
        rope_rows=math.gcd(tokens // 4, 1024),
    )


def _dot(a, b):
    return jnp.dot(a, b, preferred_element_type=F32)


def _dot_nt(a, b):
    return lax.dot_general(a, b, (((1,), (1,)), ((), ())), preferred_element_type=F32)


def _dot_tn(a, b):
    return lax.dot_general(a, b, (((0,), (0,)), ((), ())), preferred_element_type=F32)


def _rms(x, g):
    return x * lax.rsqrt(jnp.mean(x * x, axis=-1, keepdims=True) + EPS) * g


def _silu(x):
    return x * jax.nn.sigmoid(x)


def _log2_sigmoid(x):
    y = x * math.log2(math.e)
    return jnp.minimum(y, 0.0) - jnp.log2(1.0 + jnp.exp2(-jnp.abs(y)))


def _split2(x):
    hi = x.astype(BF16)
    lo = (x - hi.astype(F32)).astype(BF16)
    return hi, lo


def _rope_table_kernel(pos_ref, inv_ref, tq_ref, tkc_ref, tks_ref):
    half = MLA_ROPE // 2
    ang = pos_ref[...] * inv_ref[...]
    cos = jnp.cos(ang)
    sin = jnp.sin(ang)
    lane = lax.broadcasted_iota(jnp.int32, cos.shape, 1)
    for i in range(LANES // half):
        place = lambda x, dst: x if dst == i else pltpu.roll(x, ((dst - i) * half) % LANES, axis=1)
        base = jnp.where(lane < half, place(cos, 0),
                         jnp.where(lane < 2 * half, place(cos, 1),
                                   jnp.where(lane < 3 * half, -place(sin, 2), place(sin, 3))))
        tq_ref[i] = base * Q_SCALE
        tkc_ref[i] = jnp.where(lane < 2 * half, base, 0.0)
        tks_ref[i] = jnp.where(lane < 2 * half, pltpu.roll(base, 2 * half, axis=1), 0.0)


def _rope_tables(positions, rows):
    half = MLA_ROPE // 2
    per_row = LANES // half
    tokens = positions.size
    n = tokens // per_row
    inv = ROPE_THETA ** (-jnp.arange(half, dtype=F32) / half)
    pos = jnp.broadcast_to(positions.reshape(per_row, n, 1).astype(F32), (per_row, n, half))
    pos = pos.transpose(1, 0, 2).reshape(n, LANES)
    inv4 = jnp.tile(inv, per_row).reshape(1, LANES)
    out_spec = pl.BlockSpec((per_row, rows, LANES), lambda i: (0, i, 0))
    tabs = pl.pallas_call(
        _rope_table_kernel,
        grid=(n // rows,),
        in_specs=[pl.BlockSpec((rows, LANES), lambda i: (i, 0)), pl.BlockSpec((1, LANES), lambda i: (0, 0))],
        out_specs=[out_spec] * 3,
        out_shape=[jax.ShapeDtypeStruct((per_row, n, LANES), F32)] * 3,
        name="rope_tables",
    )(pos, inv4)
    return tuple(t.reshape(tokens, LANES) for t in tabs)


_C_CQ = 0
_C_KRLR = _C_CQ + MLA_Q_RANK
_C_CKV = _C_KRLR + LANES
_C_GA = _C_CKV + MLA_KV_RANK
_C_GQ = _C_GA + MLA_WIDTH
_C_GK = _C_GQ + GLA_QK
_C_GV = _C_GK + GLA_QK
_C_GG = _C_GV + GLA_WIDTH
_C_END = _C_GG + GLA_WIDTH
_LR_LANE = MLA_ROPE
_W_CQ = 0
_W_CKV = _W_CQ + MLA_Q_RANK
_W_KR = _W_CKV + MLA_KV_RANK
_W_GA = _W_KR + MLA_ROPE
_W_LR = _W_GA + MLA_WIDTH + 2 * GLA_QK + GLA_WIDTH
_W_GG = _W_LR + 2 * GLA_GATE_RANK
_W_END = _W_GG + GLA_WIDTH


def _win_kernel(wt_ref, o_ref):
    take = lambda a, b: wt_ref[a:b, :]
    pad = jnp.zeros((LANES - MLA_ROPE - 2 * GLA_GATE_RANK, wt_ref.shape[1]), wt_ref.dtype)
    wt = jnp.concatenate([take(_W_CQ, _W_CKV), take(_W_KR, _W_GA), take(_W_LR, _W_GG), pad,
                          take(_W_CKV, _W_KR), take(_W_GA, _W_LR), take(_W_GG, _W_END)], axis=0)
    o_ref[...] = wt.T.astype(BF16)


def _permute_win(w_in, rows):
    depth, d, n = w_in.shape
    return pl.pallas_call(
        _win_kernel,
        grid=(depth, d // rows),
        in_specs=[pl.BlockSpec((None, n, rows), lambda l, i: (l, 0, i))],
        out_specs=pl.BlockSpec((None, rows, _C_END), lambda l, i: (l, i, 0)),
        out_shape=jax.ShapeDtypeStruct((depth, d, _C_END), BF16),
        name="permute_w_in",
    )(jnp.swapaxes(w_in, 1, 2))


def _front_kernel(h_ref, ln_ref, win_ref, qn_ref, wq_ref, kvn_ref, wkv_ref,
                  wg_ref, bg_ref, tq_ref, tkc_ref, tks_ref,
                  q_ref, k_ref, v_ref, sga_ref, gf_ref, gb_ref, dec_ref, gv_ref, sgg_ref, *, chunk):
    xn = _rms(h_ref[...], ln_ref[...]).astype(BF16)
    u = _dot(xn, win_ref[...])

    def proj(c0, c1):
        return u[:, c0:c1]

    cq_krlr = proj(_C_CQ, _C_CKV)
    krlr = cq_krlr[:, _C_KRLR:_C_CKV]

    cqn = _rms(cq_krlr[:, :_C_KRLR], qn_ref[...]).astype(BF16)
    q_all = _dot(cqn, wq_ref[...])
    q_nope = q_all[:, :MLA_HEADS * MLA_NOPE] * Q_SCALE
    q_rope = q_all[:, MLA_HEADS * MLA_NOPE:]
    tq = tq_ref[...]
    for hd in range(MLA_HEADS):
        g = slice(hd * LANES, (hd + 1) * LANES)
        q_ref[:, hd * QK_PAD:hd * QK_PAD + LANES] = q_nope[:, g].astype(BF16)
        qr = q_rope[:, g] * tq
        q_ref[:, hd * QK_PAD + LANES:(hd + 1) * QK_PAD] = (qr + pltpu.roll(qr, MLA_ROPE, axis=1)).astype(BF16)

    ckvn = _rms(proj(_C_CKV, _C_GA), kvn_ref[...]).astype(BF16)
    kv = _dot(ckvn, wkv_ref[...])
    k_nope = kv[:, :MLA_HEADS * MLA_NOPE]
    v_ref[...] = kv[:, MLA_HEADS * MLA_NOPE:].astype(BF16)
    half = MLA_ROPE // 2
    lane = lax.broadcasted_iota(jnp.int32, krlr.shape, 1)
    kr_sw = jnp.where(lane < half, pltpu.roll(krlr, LANES - half, axis=1), pltpu.roll(krlr, half, axis=1))
    kr = (krlr * tkc_ref[...] + kr_sw * tks_ref[...]).astype(BF16)
    for hd in range(MLA_HEADS):
        k_ref[:, hd * QK_PAD:hd * QK_PAD + LANES] = k_nope[:, hd * LANES:(hd + 1) * LANES].astype(BF16)
        k_ref[:, hd * QK_PAD + LANES:(hd + 1) * QK_PAD] = kr

    sga_ref[...] = _silu(proj(_C_GA, _C_GQ)).astype(BF16)
    sgg_ref[...] = _silu(proj(_C_GG, _C_END)).astype(BF16)

    gq = proj(_C_GQ, _C_GK) * GLA_DK ** -0.5
    gk = proj(_C_GK, _C_GV)
    gv_ref[...] = proj(_C_GV, _C_GG).astype(BF16)
    la = _log2_sigmoid(_dot(krlr.astype(BF16), wg_ref[...]) + bg_ref[...])
    row = lax.broadcasted_iota(jnp.int32, (chunk, chunk), 0)
    col = lax.broadcasted_iota(jnp.int32, (chunk, chunk), 1)
    for d, out_ref in enumerate((gf_ref, gb_ref)):
        forward = d == 0
        tri = jnp.where((col <= row) if forward else (col >= row), 1.0 / GLA_TAU, 0.0).astype(BF16)
        for ci in range(la.shape[0] // chunk):
            r = slice(ci * chunk, (ci + 1) * chunk)
            hi, lo = _split2(la[r, d * GLA_QK:(d + 1) * GLA_QK])
            b = _dot(tri, jnp.concatenate([hi, lo], axis=1))
            b = b[:, :GLA_QK] + b[:, GLA_QK:]
            mid = chunk // 2 - 1 if forward else chunk // 2
            end = chunk - 1 if forward else 0
            b_mid = b[mid:mid + 1, :]
            b_end = b[end:end + 1, :]
            out_ref[r, :GLA_QK] = (gq[r] * jnp.exp2(b - b_mid)).astype(BF16)
            out_ref[r, GLA_QK:] = (gk[r] * jnp.exp2(b_mid - b)).astype(BF16)
            base = d * 3 * GLA_QK
            dec_ref[ci:ci + 1, base:base + GLA_QK] = jnp.exp2(b_end)
            dec_ref[ci:ci + 1, base + GLA_QK:base + 2 * GLA_QK] = jnp.exp2(b_mid)
            dec_ref[ci:ci + 1, base + 2 * GLA_QK:base + 3 * GLA_QK] = jnp.exp2(b_end - b_mid)


def _front_weights(w_in, w_uq, w_ukv, wgf, bgf, wgb, bgb):
    depth = w_in.shape[0]
    half = MLA_ROPE // 2
    swap = lambda w: jnp.concatenate([w[..., half:], w[..., :half]], axis=-1)
    win = _permute_win(w_in, math.gcd(w_in.shape[1], 256))

    wq = w_uq.reshape(depth, MLA_Q_RANK, MLA_HEADS, MLA_NOPE + MLA_ROPE)
    wqn = wq[..., :MLA_NOPE].reshape(depth, MLA_Q_RANK, MLA_HEADS * MLA_NOPE).astype(BF16)
    wr = wq[..., MLA_NOPE:]
    wqr = jnp.concatenate([wr, swap(wr)], axis=-1).reshape(depth, MLA_Q_RANK, MLA_HEADS * LANES).astype(BF16)

    wkv = w_ukv.reshape(depth, MLA_KV_RANK, MLA_HEADS, MLA_NOPE + MLA_V)
    wk = wkv[..., :MLA_NOPE].reshape(depth, MLA_KV_RANK, MLA_HEADS * MLA_NOPE).astype(BF16)
    wv = wkv[..., MLA_NOPE:].reshape(depth, MLA_KV_RANK, MLA_HEADS * MLA_V).astype(BF16)

    wg = jnp.zeros((depth, LANES, 2 * GLA_QK), F32)
    wg = wg.at[:, _LR_LANE:_LR_LANE + GLA_GATE_RANK, :GLA_QK].set(wgf)
    wg = wg.at[:, _LR_LANE + GLA_GATE_RANK:_LR_LANE + 2 * GLA_GATE_RANK, GLA_QK:].set(wgb)
    bg = jnp.concatenate([bgf, bgb], axis=-1).reshape(depth, 1, 2 * GLA_QK)
    wq = jnp.concatenate([wqn, wqr], axis=-1)
    wkv = jnp.concatenate([wk, wv], axis=-1)
    return win, wq, wkv, wg.astype(BF16), bg


def _layer_spec(a, layer):
    return pl.BlockSpec((None,) + a.shape[1:], lambda i: (layer, 0, 0))


def _front(h, ln, weights, qn, kvn, tabs, layer, tm, chunk):
    tokens, d = h.shape
    assert tm % chunk == 0, (tm, chunk)
    win, wq, wkv, wg, bg = weights
    tq, tkc, tks = tabs
    row = lambda n: pl.BlockSpec((tm, n), lambda i: (i, 0))
    full = lambda a: _layer_spec(a, layer)
    ln, qn, kvn = (a.reshape(a.shape[0], 1, a.shape[1]) for a in (ln, qn, kvn))
    outs = [
        (MLA_HEADS * QK_PAD, BF16),
        (MLA_HEADS * QK_PAD, BF16),
        (MLA_WIDTH, BF16),
        (MLA_WIDTH, BF16),
        (2 * GLA_QK, BF16),
        (2 * GLA_QK, BF16),
        None,
        (GLA_WIDTH, BF16),
        (GLA_WIDTH, BF16),
    ]
    dec_spec = pl.BlockSpec((tm // chunk, GLA_DEC), lambda i: (i, 0))
    dec_shape = jax.ShapeDtypeStruct((tokens // chunk, GLA_DEC), F32)
    return pl.pallas_call(
        functools.partial(_front_kernel, chunk=chunk),
        grid=(tokens // tm,),
        in_specs=[row(d), full(ln), full(win), full(qn), full(wq), full(kvn),
                  full(wkv), full(wg), full(bg), row(LANES), row(LANES), row(LANES)],
        out_specs=[dec_spec if o is None else row(o[0]) for o in outs],
        out_shape=[dec_shape if o is None else jax.ShapeDtypeStruct((tokens, o[0]), o[1]) for o in outs],
        compiler_params=pltpu.CompilerParams(dimension_semantics=("arbitrary",), vmem_limit_bytes=VMEM_LIMIT),
        name="front",
    )(h, ln, win, qn, wq, kvn, wkv, wg, bg, tq, tkc, tks)


def _attn_kernel(q_ref, k_ref, v_ref, g_ref, o_ref, s_ref, m_ref, vext_ref, *, tq):
    vext_ref[:, :MLA_V] = v_ref[0]
    vext_ref[:, MLA_V:] = jnp.ones((v_ref.shape[1], MLA_V), BF16)
    groups = k_ref.shape[1] // LANES
    once = jnp.minimum(pl.program_id(0) + 1, 1)

    @pl.loop(0, q_ref.shape[1] // tq)
    def _(qi):
        rows = pl.ds(pl.multiple_of(qi * tq, tq), tq)

        @pl.loop(0, once)
        def _(_):
            s = _dot_nt(q_ref[0, rows, :], k_ref[0])
            s_ref[...] = s
            m_acc = s[:, :LANES]
            for c in range(1, groups):
                m_acc = jnp.maximum(m_acc, s[:, c * LANES:(c + 1) * LANES])
            m_ref[...] = jnp.broadcast_to(jnp.max(m_acc, axis=-1, keepdims=True), (tq, LANES))

        m_b = m_ref[...]

        @pl.loop(0, once)
        def _(_):
            cols = [jnp.exp2((s_ref[:, c * LANES:(c + 1) * LANES] - m_b).astype(BF16)) for c in range(groups)]
            acc = _dot(jnp.concatenate(cols, axis=1), vext_ref[...])
            o_ref[0, rows, :] = (acc[:, :MLA_V] / acc[:, MLA_V:] * g_ref[0, rows, :].astype(F32)).astype(o_ref.dtype)


def _attention(q, k, v, sgate, tq):
    b, s, _ = q.shape
    head = lambda w: pl.BlockSpec((1, s, w), lambda bi, hi: (bi, 0, hi))
    return pl.pallas_call(
        functools.partial(_attn_kernel, tq=tq),
        grid=(b, MLA_HEADS),
        in_specs=[head(QK_PAD), head(QK_PAD), head(MLA_V), head(MLA_V)],
        out_specs=head(MLA_V),
        out_shape=jax.ShapeDtypeStruct((b, s, MLA_WIDTH), BF16),
        scratch_shapes=[pltpu.VMEM((tq, s), F32), pltpu.VMEM((tq, LANES), F32), pltpu.VMEM((s, 2 * MLA_V), BF16)],
        compiler_params=pltpu.CompilerParams(
            dimension_semantics=("arbitrary", "arbitrary"), vmem_limit_bytes=VMEM_LIMIT),
        name="mla_attention",
    )(q, k, v, sgate)


def _gla_direction(g_ref, v_ref, dec_ref, state_ref, forward, chunk, emit):
    c = chunk
    n_sub = g_ref.shape[1] // c
    row = lax.broadcasted_iota(jnp.int32, (c, 2 * c), 0)
    col = lax.broadcasted_iota(jnp.int32, (c, 2 * c), 1) % c
    tri2 = (col <= row) if forward else (col >= row)
    lane_head = lax.broadcasted_iota(jnp.int32, (c, LANES), 1) // GLA_DK
    own = (lax.broadcasted_iota(jnp.int32, (2 * GLA_DV, LANES), 0) // GLA_DV
           == lax.broadcasted_iota(jnp.int32, (2 * GLA_DV, LANES), 1) // GLA_DK)
    zero = jnp.zeros((c, LANES), BF16)
    base = 0 if forward else 3 * GLA_QK
    state_ref[...] = jnp.zeros_like(state_ref)
    for sub in (range(n_sub) if forward else reversed(range(n_sub))):
        r = slice(sub * c, (sub + 1) * c)
        for pr in range(GLA_HEADS // 2):
            pair = pr * LANES
            fac = lambda i: dec_ref[sub:sub + 1, base + i * GLA_QK + pair:base + i * GLA_QK + pair + LANES]
            srows = slice(2 * pr * GLA_DV, (2 * pr + 2) * GLA_DV)
            q_in = g_ref[0, r, pair:pair + LANES]
            k_pair = g_ref[0, r, GLA_QK + pair:GLA_QK + pair + LANES]
            k2 = jnp.concatenate([jnp.where(lane_head == 0, k_pair, zero),
                                  jnp.where(lane_head == 1, k_pair, zero)], axis=0)
            a2 = jnp.where(tri2, _dot_nt(q_in, k2), 0.0).astype(BF16)
            v2 = v_ref[0, r, 2 * pr * GLA_DV:(2 * pr + 2) * GLA_DV]
            v_bd = jnp.concatenate([jnp.concatenate([v2[:, :GLA_DV], zero], axis=1),
                                    jnp.concatenate([zero, v2[:, GLA_DV:]], axis=1)], axis=0)
            st2 = state_ref[srows, :]
            emit(r, pr, _dot(a2, v_bd) + _dot_nt(q_in, (st2 * fac(1)).astype(BF16)))
            state_ref[srows, :] = st2 * fac(0) + jnp.where(own, _dot_tn(v2, k_pair), 0.0) * fac(2)


def _gla_kernel(gf_ref, gb_ref, v_ref, dec_ref, sgg_ref, on_ref, y_ref, state_ref, *, chunk):
    cols = lambda pr, j: slice((2 * pr + j) * GLA_DV, (2 * pr + j + 1) * GLA_DV)

    def keep_forward(r, pr, o):
        y_ref[0, r, 2 * pr * GLA_DV:(2 * pr + 2) * GLA_DV] = o.astype(y_ref.dtype)

    def combine(r, pr, o):
        for j in range(2):
            og = y_ref[0, r, cols(pr, j)].astype(F32) + o[:, j * GLA_DV:(j + 1) * GLA_DV]
            gate = sgg_ref[0, r, cols(pr, j)].astype(F32)
            y_ref[0, r, cols(pr, j)] = (_rms(og, on_ref[...]) * gate).astype(y_ref.dtype)

    _gla_direction(gf_ref, v_ref, dec_ref, state_ref, True, chunk, keep_forward)
    _gla_direction(gb_ref, v_ref, dec_ref, state_ref, False, chunk, combine)


def _gla(gf, gb, gv, dec, sgg, out_norm, layer, chunk):
    b, s, _ = gv.shape
    n = s // chunk
    seq = lambda w: pl.BlockSpec((1, s, w), lambda bi: (bi, 0, 0))
    on = out_norm.reshape(out_norm.shape[0], 1, out_norm.shape[1])
    return pl.pallas_call(
        functools.partial(_gla_kernel, chunk=chunk),
        grid=(b,),
        in_specs=[seq(2 * GLA_QK), seq(2 * GLA_QK), seq(GLA_WIDTH),
                  pl.BlockSpec((n, GLA_DEC), lambda bi: (bi, 0)), seq(GLA_WIDTH), _layer_spec(on, layer)],
        out_specs=seq(GLA_WIDTH),
        out_shape=jax.ShapeDtypeStruct((b, s, GLA_WIDTH), BF16),
        scratch_shapes=[pltpu.VMEM((GLA_HEADS * GLA_DV, LANES), F32)],
        compiler_params=pltpu.CompilerParams(dimension_semantics=("arbitrary",), vmem_limit_bytes=VMEM_LIMIT_SCAN),
        name="gla_scan",
    )(gf, gb, gv, dec, sgg, on)


def _back_kernel(h_ref, ym_ref, yg_ref, wout_ref, pn_ref, wpg_ref, p_ref, wpp_ref, fn_ref, o_ref, *, last):
    y = jnp.concatenate([ym_ref[...], yg_ref[...]], axis=1)
    h1 = h_ref[...] + _dot(y, wout_ref[...])
    gate = jax.nn.sigmoid(_dot(_rms(h1, pn_ref[...]).astype(BF16), wpg_ref[...]))
    h2 = h1 + gate * _dot(p_ref[0].astype(BF16), wpp_ref[...])
    o_ref[...] = _rms(h2, fn_ref[...]) if last else h2


def _back(h, y_mla, y_gla, w_out, ple_norm, w_pg, p, layer, w_pp, final_norm, tm, last):
    tokens, d = h.shape
    row = lambda n: pl.BlockSpec((tm, n), lambda i: (i, 0))
    full = lambda a: _layer_spec(a, layer)
    p_spec = pl.BlockSpec((1, tm, p.shape[2]), lambda i: (layer, i, 0))
    pn = ple_norm.reshape(ple_norm.shape[0], 1, ple_norm.shape[1])
    fn = final_norm.reshape(1, -1)
    return pl.pallas_call(
        functools.partial(_back_kernel, last=last),
        grid=(tokens // tm,),
        in_specs=[row(d), row(MLA_WIDTH), row(GLA_WIDTH), full(w_out), full(pn), full(w_pg), p_spec, full(w_pp),
                  pl.BlockSpec(fn.shape, lambda i: (0, 0))],
        out_specs=row(d),
        out_shape=jax.ShapeDtypeStruct((tokens, d), F32),
        compiler_params=pltpu.CompilerParams(dimension_semantics=("arbitrary",), vmem_limit_bytes=VMEM_LIMIT),
        name="back",
    )(h, y_mla, y_gla, w_out, pn, w_pg, p, w_pp, fn)


def kernel(x, p, positions, ln_mix, w_in, mla_q_norm, w_uq, mla_kv_norm, w_ukv, gla_w_gate_fwd, gla_b_gate_fwd,
           gla_w_gate_bwd, gla_b_gate_bwd, gla_out_norm, w_out, ple_norm, w_ple_gate, w_ple_proj, final_norm):
    batch, seq, d = x.shape
    depth = w_in.shape[0]
    tokens = batch * seq
    t = _tiles(batch, seq)
    tabs = _rope_tables(positions, t["rope_rows"])
    h = x.reshape(tokens, d)
    p_all = p.reshape(depth, tokens, p.shape[-1])
    seq3 = lambda a: a.reshape(batch, seq, a.shape[-1])
    weights = _front_weights(w_in, w_uq, w_ukv, gla_w_gate_fwd, gla_b_gate_fwd, gla_w_gate_bwd, gla_b_gate_bwd)
    w_out, w_ple_gate, w_ple_proj = (w.astype(BF16) for w in (w_out, w_ple_gate, w_ple_proj))
    for i in range(depth):
        q, k, v, sga, gf, gb, dec, gv, sgg = _front(
            h, ln_mix, weights, mla_q_norm, mla_kv_norm, tabs, i, t["row_tile"], t["gla_chunk"])
        y_mla = _attention(seq3(q), seq3(k), seq3(v), seq3(sga), t["q_tile"])
        y_gla = _gla(seq3(gf), seq3(gb), seq3(gv), dec, seq3(sgg), gla_out_norm, i, t["gla_chunk"])
        h = _back(h, y_mla.reshape(tokens, -1), y_gla.reshape(tokens, -1), w_out, ple_norm, w_ple_gate, p_all, i,
                  w_ple_proj, final_norm, t["back_tile"], last=(i == depth - 1))
    return h.reshape(batch, seq, d)
```
